```python
import math
import jax, jax.numpy as jnp
from jax import lax
import numpy as np

D_MODEL = 1024
BATCH = 4
SEQ = 8192
DEPTH = 1

N_META = 16
GDN_HEADS = 4
GDN_DK = 128
GDN_DV = 128
GDN_CONV = 4
GDN_CHUNK = 64
GDN_QK = GDN_HEADS * GDN_DK
D_GDN = GDN_HEADS * GDN_DV
GDN_CONV_CH = 2 * GDN_QK + D_GDN
SB_HEADS = 8
SB_DH = 64
SB_BLOCK = 128
D_SB = SB_HEADS * SB_DH
D_MIX = D_GDN + D_SB
IN_SPLIT_SIZES = (GDN_QK, GDN_QK, D_GDN, GDN_HEADS, GDN_HEADS, D_GDN, D_SB, D_SB, D_SB)
D_IN = 2 * GDN_QK + 2 * D_GDN + 2 * GDN_HEADS + 3 * D_SB
D_FF = 2816
FFN_CONV = 3
NORM_EPS = 1e-6

kernel_name = "hymba_gdn_stickbreak_convffn_layer"


def rmsnorm(x, w, eps=NORM_EPS):
    xf = x.astype(jnp.float32)
    y = xf * lax.rsqrt(jnp.mean(xf * xf, axis=-1, keepdims=True) + eps)
    return (y * w.astype(jnp.float32)).astype(x.dtype)


def l2norm(x, eps=1e-6):
    return x * lax.rsqrt(jnp.sum(x * x, axis=-1, keepdims=True) + eps)


def causal_dwconv(x, w):
    width, ch = w.shape
    return lax.conv_general_dilated(
        x, w[:, None, :].astype(x.dtype), window_strides=(1,), padding=[(width - 1, 0)],
        dimension_numbers=("NWC", "WIO", "NWC"), feature_group_count=ch)


def gated_deltanet(q, k, v, a, b, z, conv_w, A_log, dt_bias, norm_w):
    Bsz, L, _ = q.shape
    H, C = GDN_HEADS, GDN_CHUNK
    qkv = jax.nn.silu(causal_dwconv(jnp.concatenate([q, k, v], axis=-1), conv_w)).astype(jnp.float32)
    q, k, v = jnp.split(qkv, [GDN_QK, 2 * GDN_QK], axis=-1)
    q = l2norm(q.reshape(Bsz, L, H, GDN_DK)) * (GDN_DK ** -0.5)
    k = l2norm(k.reshape(Bsz, L, H, GDN_DK))
    v = v.reshape(Bsz, L, H, GDN_DV)
    beta = jax.nn.sigmoid(b.astype(jnp.float32))
    g = -jnp.exp(A_log.astype(jnp.float32)) * jax.nn.softplus(a.astype(jnp.float32) + dt_bias.astype(jnp.float32))

    pad = (-L) % C
    pad4 = ((0, 0), (pad, 0), (0, 0), (0, 0))
    pad3 = ((0, 0), (pad, 0), (0, 0))
    q, k, v = jnp.pad(q, pad4), jnp.pad(k, pad4), jnp.pad(v, pad4)
    beta, g = jnp.pad(beta, pad3), jnp.pad(g, pad3)
    Lp = L + pad
    N = Lp // C
    chunk4 = lambda t: t.reshape(Bsz, N, C, H, t.shape[-1]).transpose(0, 3, 1, 2, 4)
    chunk3 = lambda t: t.reshape(Bsz, N, C, H).transpose(0, 3, 1, 2)
    qc, kc, vc = chunk4(q), chunk4(k), chunk4(v)
    bc, G = chunk3(beta), jnp.cumsum(chunk3(g), axis=-1)

    incl = jnp.tril(jnp.ones((C, C), bool))
    strict = jnp.tril(jnp.ones((C, C), bool), -1)
    decay = jnp.exp(jnp.where(incl, G[..., :, None] - G[..., None, :], -jnp.inf))
    kb = kc * bc[..., None]
    A = jnp.where(strict, jnp.einsum("bhnid,bhnjd->bhnij", kb, kc) * decay, 0.0)
    eye = jnp.broadcast_to(jnp.eye(C, dtype=jnp.float32), A.shape)
    T = lax.linalg.triangular_solve(eye + A, eye, left_side=True, lower=True, unit_diagonal=True)
    u = jnp.einsum("bhnij,bhnjd->bhnid", T, vc * bc[..., None])
    w = jnp.einsum("bhnij,bhnjd->bhnid", T, kb * jnp.exp(G)[..., None])
    qk = jnp.einsum("bhnid,bhnjd->bhnij", qc, kc) * decay
    q_dec = qc * jnp.exp(G)[..., None]
    k_dec = kc * jnp.exp(G[..., -1:] - G)[..., None]
    g_last = jnp.exp(G[..., -1])

    def step(S, inp):
        qk_n, u_n, w_n, qd_n, kd_n, gl_n = inp
        v_new = u_n - jnp.einsum("bhcd,bhde->bhce", w_n, S)
        o = jnp.einsum("bhcd,bhde->bhce", qd_n, S) + jnp.einsum("bhcs,bhse->bhce", qk_n, v_new)
        S = S * gl_n[..., None, None] + jnp.einsum("bhcd,bhce->bhde", kd_n, v_new)
        return S, o

    mv = lambda t: jnp.moveaxis(t, 2, 0)
    S0 = jnp.zeros((Bsz, H, GDN_DK, GDN_DV), jnp.float32)
    _, o = lax.scan(step, S0, (mv(qk), mv(u), mv(w), mv(q_dec), mv(k_dec), jnp.moveaxis(g_last, 2, 0)))
    o = o.transpose(1, 0, 3, 2, 4).reshape(Bsz, Lp, H, GDN_DV)[:, pad:]
    zg = jax.nn.silu(z.astype(jnp.float32).reshape(Bsz, L, H, GDN_DV))
    o = rmsnorm(o, norm_w) * zg
    return o.reshape(Bsz, L, D_GDN).astype(z.dtype)


def stick_breaking(q, k, v, norm_w):
    Bsz, L, _ = q.shape
    out_dtype = v.dtype
    H, Dh, BLK = SB_HEADS, SB_DH, SB_BLOCK
    pad = (-L) % BLK
    heads = lambda t: jnp.pad(t.astype(jnp.float32).reshape(Bsz, L, H, Dh), ((0, 0), (pad, 0), (0, 0), (0, 0))).transpose(0, 2, 1, 3)
    q, k, v = heads(q), heads(k), heads(v)
    Lp = L + pad
    nblk = Lp // BLK
    key_pos = jnp.arange(Lp)
    key_real = key_pos >= pad
    qb = q.reshape(Bsz, H, nblk, BLK, Dh).transpose(2, 0, 1, 3, 4)
    scale = Dh ** -0.5

    def block(args):
        q_blk, bi = args
        q_pos = bi * BLK + jnp.arange(BLK)
        zs = jnp.einsum("bhqd,bhkd->bhqk", q_blk, k) * scale
        mask = (key_pos[None, :] < q_pos[:, None]) & key_real[None, :]
        log_1m = jnp.where(mask, jax.nn.log_sigmoid(-zs), 0.0)
        log_surv = lax.cumsum(log_1m, axis=3, reverse=True) - log_1m
        att = jnp.exp(jnp.where(mask, jax.nn.log_sigmoid(zs) + log_surv, -jnp.inf))
        return jnp.einsum("bhqk,bhkd->bhqd", att, v)

    o = lax.map(block, (qb, jnp.arange(nblk)))
    o = o.transpose(1, 0, 3, 2, 4).reshape(Bsz, Lp, H, Dh)[:, pad:]
    o = rmsnorm(o, norm_w)
    return o.reshape(Bsz, L, D_SB).astype(out_dtype)


def conv_glu_ffn(x, w_up, conv_w, conv_b, w_down):
    h = causal_dwconv(x @ w_up, conv_w) + conv_b
    gate, up = jnp.split(h, 2, axis=-1)
    return (jax.nn.gelu(gate, approximate=True) * up) @ w_down


def setup_inputs(seed: int = 0) -> dict:
    key = jax.random.key(seed)
    ks = jax.random.split(key, 20)
    f32 = jnp.float32
    nrm = lambda k, shape, s: jax.random.normal(k, shape, f32) * s
    dt = jnp.exp(jax.random.uniform(ks[5], (DEPTH, GDN_HEADS), f32, math.log(1e-3), math.log(1e-1)))
    return {
        "x": nrm(ks[0], (BATCH, SEQ, D_MODEL), 1.0),
        "meta_tokens": nrm(ks[1], (N_META, D_MODEL), 1.0),
        "attn_pre_norm": 1.0 + nrm(ks[2], (DEPTH, D_MODEL), 0.02),
        "w_in": nrm(ks[3], (DEPTH, D_MODEL, D_IN), D_MODEL ** -0.5),
        "gdn_conv_w": nrm(ks[4], (DEPTH, GDN_CONV, GDN_CONV_CH), GDN_CONV ** -0.5),
        "gdn_A_log": jnp.log(jax.random.uniform(ks[6], (DEPTH, GDN_HEADS), f32, 1.0, 16.0)),
        "gdn_dt_bias": dt + jnp.log(-jnp.expm1(-dt)),
        "gdn_norm_w": 1.0 + nrm(ks[7], (DEPTH, GDN_DV), 0.02),
        "sb_norm_w": 1.0 + nrm(ks[8], (DEPTH, SB_HEADS, SB_DH), 0.02),
        "w_out": nrm(ks[9], (DEPTH, D_MIX, D_MODEL), D_MIX ** -0.5),
        "attn_post_norm": 1.0 + nrm(ks[10], (DEPTH, D_MODEL), 0.02),
        "ffn_pre_norm": 1.0 + nrm(ks[11], (DEPTH, D_MODEL), 0.02),
        "w_ffn_up": nrm(ks[12], (DEPTH, D_MODEL, 2 * D_FF), D_MODEL ** -0.5),
        "ffn_conv_w": nrm(ks[13], (DEPTH, FFN_CONV, 2 * D_FF), FFN_CONV ** -0.5),
        "ffn_conv_b": nrm(ks[14], (DEPTH, 2 * D_FF), 0.01),
        "w_ffn_down": nrm(ks[15], (DEPTH, D_FF, D_MODEL), D_FF ** -0.5),
        "ffn_post_norm": 1.0 + nrm(ks[16], (DEPTH, D_MODEL), 0.02),
    }


def reference(x, meta_tokens, attn_pre_norm, w_in, gdn_conv_w, gdn_A_log, gdn_dt_bias, gdn_norm_w,
              sb_norm_w, w_out, attn_post_norm, ffn_pre_norm, w_ffn_up, ffn_conv_w, ffn_conv_b,
              w_ffn_down, ffn_post_norm):
    Bsz = x.shape[0]
    meta = jnp.broadcast_to(meta_tokens[None].astype(x.dtype), (Bsz, N_META, D_MODEL))
    h = jnp.concatenate([meta, x], axis=1)
    split_idx = np.cumsum(IN_SPLIT_SIZES)[:-1].tolist()
    for l in range(DEPTH):
        u = rmsnorm(h, attn_pre_norm[l])
        q_g, k_g, v_g, a_g, b_g, z_g, q_s, k_s, v_s = jnp.split(u @ w_in[l], split_idx, axis=-1)
        y_g = gated_deltanet(q_g, k_g, v_g, a_g, b_g, z_g, gdn_conv_w[l], gdn_A_log[l], gdn_dt_bias[l], gdn_norm_w[l])
        y_s = stick_breaking(q_s, k_s, v_s, sb_norm_w[l])
        mix = jnp.concatenate([y_g, y_s], axis=-1) @ w_out[l]
        h = h + rmsnorm(mix, attn_post_norm[l])
        f = conv_glu_ffn(rmsnorm(h, ffn_pre_norm[l]), w_ffn_up[l], ffn_conv_w[l], ffn_conv_b[l], w_ffn_down[l])
        h = h + rmsnorm(f, ffn_post_norm[l])
    return h[:, N_META:]
```

```python
import functools
import math

import jax
import jax.numpy as jnp
from jax import lax
from jax.experimental import pallas as pl
from jax.experimental.pallas import tpu as pltpu

F32 = jnp.float32
BF16 = jnp.bfloat16

N_META = 16
GDN_HEADS = 4
GDN_DK = 128
GDN_DV = 128
GDN_CONV = 4
GDN_CHUNK = 64
GDN_QK = GDN_HEADS * GDN_DK
D_GDN = GDN_HEADS * GDN_DV
GDN_CONV_CH = 2 * GDN_QK + D_GDN
SB_HEADS = 8
SB_DH = 64
D_SB = SB_HEADS * SB_DH
FFN_CONV = 3
NORM_EPS = 1e-6
L2_EPS = 1e-6

LANES = 128
SB_TILE = 256
ROW_ALIGN = SB_TILE
HALO = 16
NEG_BIG = -1e30
VMEM_LIMIT = 56 * 1024 * 1024


def _pick_tile(n, candidates):
    for c in candidates:
        if n % c == 0:
            return c
    raise ValueError(f"no tile for {n}")


def _dot(a, b):
    return jnp.dot(a, b, preferred_element_type=F32)


def _dot_nt(a, b):
    return lax.dot_general(a, b, (((1,), (1,)), ((), ())), preferred_element_type=F32)


def _dot_tn(a, b):
    return lax.dot_general(a, b, (((0,), (0,)), ((), ())), preferred_element_type=F32)


def _split2(x):
    hi = x.astype(BF16)
    lo = (x - hi.astype(F32)).astype(BF16)
    return hi, lo


def _dot_split(a, b):
    ah, al = _split2(a)
    bh, bl = _split2(b)
    return _dot(ah, bh) + (_dot(ah, bl) + _dot(al, bh))


def _silu(x):
    return x * jax.nn.sigmoid(x)


def _softplus(x):
    return jnp.maximum(x, 0.0) + jnp.log(1.0 + jnp.exp(-jnp.abs(x)))


def _const_spec(shape):
    nd = len(shape)
    return pl.BlockSpec(shape, lambda *_: (0,) * nd, pipeline_mode=pl.Buffered(1))


def _in_proj_kernel(h_ref, nw_ref, w_ref, wab_ref, qkvg_ref, z_ref, qs_ref, ks_ref, vs_ref, ab_ref, u_scr):
    h = h_ref[...]
    ms = jnp.mean(h * h, axis=-1, keepdims=True)
    u_scr[...] = (h * lax.rsqrt(ms + NORM_EPS) * nw_ref[...]).astype(BF16)
    u = u_scr[...]
    c0 = 0
    qkvg_ref[...] = _dot(u, w_ref[:, c0:c0 + GDN_CONV_CH])
    c0 += GDN_CONV_CH
    z_ref[...] = _dot(u, w_ref[:, c0:c0 + D_GDN])
    c0 += D_GDN
    qs_ref[...] = (_dot(u, w_ref[:, c0:c0 + D_SB]) * (SB_DH ** -0.5)).astype(BF16)
    c0 += D_SB
    ks_ref[...] = _dot(u, w_ref[:, c0:c0 + D_SB]).astype(BF16)
    c0 += D_SB
    vs_ref[...] = _dot(u, w_ref[:, c0:c0 + D_SB]).astype(BF16)
    ab_ref[...] = _dot(u, wab_ref[...])


def _in_proj(hp, norm_w, w_main, w_ab, tm):
    rows, d = hp.shape
    grid = (rows // tm,)
    row_spec = lambda n: pl.BlockSpec((tm, n), lambda i: (i, 0))
    return pl.pallas_call(
        _in_proj_kernel,
        grid=grid,
        in_specs=[row_spec(d), _const_spec((1, d)), _const_spec(w_main.shape), _const_spec(w_ab.shape)],
        out_specs=[row_spec(GDN_CONV_CH), row_spec(D_GDN), row_spec(D_SB), row_spec(D_SB), row_spec(D_SB),
                   row_spec(LANES)],
        out_shape=[jax.ShapeDtypeStruct((rows, GDN_CONV_CH), F32), jax.ShapeDtypeStruct((rows, D_GDN), F32),
                   jax.ShapeDtypeStruct((rows, D_SB), BF16), jax.ShapeDtypeStruct((rows, D_SB), BF16),
                   jax.ShapeDtypeStruct((rows, D_SB), BF16), jax.ShapeDtypeStruct((rows, LANES), F32)],
        scratch_shapes=[pltpu.VMEM((tm, d), BF16)],
        compiler_params=pltpu.CompilerParams(dimension_semantics=("parallel",), vmem_limit_bytes=VMEM_LIMIT),
        name="in_proj",
    )(hp, norm_w, w_main, w_ab)


def _gdn_kernel(qkv_ref, ab_ref, z_ref, cw_ref, alog_ref, dtb_ref, nw_ref, y_ref,
                xbuf, qkvc, g_scr, beta_scr, o_scr, s_scr, *, pad, tg):
    i = pl.program_id(1)
    pair = 2 * GDN_CHUNK
    c = GDN_CHUNK

    @pl.when(i == 0)
    def _():
        s_scr[...] = jnp.zeros_like(s_scr)
        xbuf[0:HALO, :] = jnp.zeros((HALO, GDN_CONV_CH), F32)

    xbuf[HALO:HALO + tg, :] = qkv_ref[0]
    for grp in range(GDN_CONV_CH // LANES):
        cols = slice(grp * LANES, (grp + 1) * LANES)
        xe = xbuf[:, cols]
        acc = xe * cw_ref[GDN_CONV - 1:GDN_CONV, cols]
        for sh in range(1, GDN_CONV):
            acc = acc + pltpu.roll(xe, sh, axis=0) * cw_ref[GDN_CONV - 1 - sh:GDN_CONV - sh, cols]
        y = _silu(acc[HALO:, :])
        if grp < 2 * GDN_HEADS:
            y = y * lax.rsqrt(jnp.sum(y * y, axis=-1, keepdims=True) + L2_EPS)
            if grp < GDN_HEADS:
                y = y * (GDN_DK ** -0.5)
        qkvc[:, cols] = y
    xbuf[0:HALO, :] = xbuf[tg:tg + HALO, :]

    ab = ab_ref[0]
    rows = i * tg + lax.broadcasted_iota(jnp.int32, (tg, LANES), 0)
    valid = rows >= pad
    g = -jnp.exp(alog_ref[...]) * _softplus(ab + dtb_ref[...])
    g_scr[...] = jnp.where(valid, g, 0.0)
    beta_scr[...] = jnp.where(valid, jax.nn.sigmoid(pltpu.roll(ab, LANES - GDN_HEADS, axis=1)), 0.0)

    ri = lax.broadcasted_iota(jnp.int32, (pair, pair), 0)
    ci = lax.broadcasted_iota(jnp.int32, (pair, pair), 1)
    same_chunk = jnp.right_shift(ri, int(math.log2(c))) == jnp.right_shift(ci, int(math.log2(c)))
    cum_mat = jnp.where((ci <= ri) & same_chunk, 1.0, 0.0).astype(BF16)
    ii = lax.broadcasted_iota(jnp.int32, (c, c), 0)
    jj = lax.broadcasted_iota(jnp.int32, (c, c), 1)
    incl = ii >= jj
    strict = ii > jj
    eye = jnp.where(ii == jj, 1.0, 0.0)

    def pair_body(p, carry):
        r0 = pl.multiple_of(p * pair, pair)
        gp = g_scr[pl.ds(r0, pair), :]
        g_hi = gp.astype(BF16)
        g_r1 = gp - g_hi.astype(F32)
        g_mid = g_r1.astype(BF16)
        g_lo = (g_r1 - g_mid.astype(F32)).astype(BF16)
        gc = _dot(cum_mat, g_hi) + (_dot(cum_mat, g_mid) + _dot(cum_mat, g_lo))
        gct = gc.T
        bp = beta_scr[pl.ds(r0, pair), :]
        for s in range(2):
            rs = slice(s * c, (s + 1) * c)
            rows_s = pl.ds(r0 + s * c, c)
            for hd in range(GDN_HEADS):
                g_col = gc[rs, hd:hd + 1]
                g_row = gct[hd:hd + 1, rs]
                g_last = gc[s * c + c - 1:s * c + c, hd:hd + 1]
                b_col = bp[rs, hd:hd + 1]
                qc = qkvc[rows_s, hd * GDN_DK:(hd + 1) * GDN_DK]
                kc = qkvc[rows_s, GDN_QK + hd * GDN_DK:GDN_QK + (hd + 1) * GDN_DK]
                vc = qkvc[rows_s, 2 * GDN_QK + hd * GDN_DV:2 * GDN_QK + (hd + 1) * GDN_DV]
                decay = jnp.exp(jnp.where(incl, g_col - g_row, NEG_BIG))
                eg = jnp.exp(g_col)
                kb = kc * b_col
                kc16 = kc.astype(BF16)
                a_mat = jnp.where(strict, _dot_nt(kb.astype(BF16), kc16) * decay, 0.0)
                pw = -a_mat
                t_mat = eye + pw
                for _ in range(int(math.log2(c)) - 1):
                    pw = _dot_split(pw, pw)
                    t_mat = t_mat + _dot_split(t_mat, pw)
                t16 = t_mat.astype(BF16)
                u = _dot(t16, (vc * b_col).astype(BF16))
                w = _dot(t16, (kb * eg).astype(BF16))
                qk = _dot_nt(qc.astype(BF16), kc16) * decay
                q_dec = qc * eg
                k_dec = kc * jnp.exp(g_last - g_col)
                s_old = s_scr[hd]
                s16 = s_old.astype(BF16)
                v_new = u - _dot(w.astype(BF16), s16)
                v16 = v_new.astype(BF16)
                o = _dot(q_dec.astype(BF16), s16) + _dot(qk.astype(BF16), v16)
                s_scr[hd] = s_old * jnp.exp(g_last) + _dot_tn(k_dec.astype(BF16), v16)
                o_scr[rows_s, hd * GDN_DV:(hd + 1) * GDN_DV] = o
        return carry

    lax.fori_loop(0, tg // pair, pair_body, 0)

    for hd in range(GDN_HEADS):
        cols = slice(hd * GDN_DV, (hd + 1) * GDN_DV)
        o = o_scr[:, cols]
        ms = jnp.mean(o * o, axis=-1, keepdims=True)
        y = o * lax.rsqrt(ms + NORM_EPS) * nw_ref[...]
        y_ref[0, :, cols] = (y * _silu(z_ref[0, :, cols])).astype(BF16)


def _gdn(qkv, ab, z, conv_w, alog_row, dtb_row, norm_row, pad, tg):
    b, lp, _ = qkv.shape
    grid = (b, lp // tg)
    tile = lambda n: pl.BlockSpec((1, tg, n), lambda bi, i: (bi, i, 0))
    kern = functools.partial(_gdn_kernel, pad=pad, tg=tg)
    return pl.pallas_call(
        kern,
        grid=grid,
        in_specs=[tile(GDN_CONV_CH), tile(LANES), tile(D_GDN), _const_spec(conv_w.shape),
                  _const_spec((1, LANES)), _const_spec((1, LANES)), _const_spec((1, GDN_DV))],
        out_specs=tile(D_GDN),
        out_shape=jax.ShapeDtypeStruct((b, lp, D_GDN), BF16),
        scratch_shapes=[pltpu.VMEM((tg + HALO, GDN_CONV_CH), F32),
                        pltpu.VMEM((tg, GDN_CONV_CH), F32),
                        pltpu.VMEM((tg, LANES), F32),
                        pltpu.VMEM((tg, LANES), F32),
                        pltpu.VMEM((tg, D_GDN), F32),
                        pltpu.VMEM((GDN_HEADS, GDN_DK, GDN_DV), F32)],
        compiler_params=pltpu.CompilerParams(dimension_semantics=("arbitrary", "arbitrary"),
                                             vmem_limit_bytes=VMEM_LIMIT),
        name="gdn",
    )(qkv, ab, z, conv_w, alog_row, dtb_row, norm_row)


def _sb_kernel(q_ref, k_ref, v_ref, nw_ref, tri_ref, y_ref, acc_scr, run_scr, *, pad):
    i = pl.program_id(2)
    t = SB_TILE
    lane = lax.broadcasted_iota(jnp.int32, (t, LANES), 1)
    first = lane < SB_DH
    q = q_ref[0]
    q_heads = (jnp.where(first, q, jnp.zeros_like(q)), jnp.where(first, jnp.zeros_like(q), q))
    tri = tri_ref[...]
    acc_scr[...] = jnp.zeros_like(acc_scr)
    run_scr[...] = jnp.zeros_like(run_scr)
    qi = lax.broadcasted_iota(jnp.int32, (t, t), 0)
    ki = lax.broadcasted_iota(jnp.int32, (t, t), 1)

    def block(j, mask):
        r0 = pl.multiple_of(j * t, t)
        kb = k_ref[0, pl.ds(r0, t), :]
        vb = v_ref[0, pl.ds(r0, t), :]
        for hd in range(2):
            z = _dot_nt(q_heads[hd], kb)
            l1m = jnp.minimum(-z, 0.0) - jnp.log(1.0 + jnp.exp(-jnp.abs(z)))
            if mask is not None:
                l1m = jnp.where(mask, l1m, 0.0)
            hi, lo = _split2(l1m)
            run = run_scr[hd]
            e = z + (_dot(hi, tri) + _dot(lo, tri) + jnp.concatenate([run, run], axis=1))
            if mask is not None:
                e = jnp.where(mask, e, NEG_BIG)
            att = jnp.exp(e).astype(BF16)
            acc_scr[hd] += _dot(att, vb)
            run_scr[hd] = run + jnp.sum(l1m, axis=1, keepdims=True)

    block(i, (ki < qi) & (ki + i * t >= pad))

    def interior(n, carry):
        block(i - 1 - n, None)
        return carry

    lax.fori_loop(0, jnp.maximum(i - 1, 0), interior, 0)

    @pl.when(i >= 1)
    def _():
        block(0, ki >= pad)

    o = jnp.where(first, acc_scr[0], acc_scr[1])
    sq = o * o
    ss_first = jnp.sum(jnp.where(first, sq, 0.0), axis=-1, keepdims=True)
    ss_all = jnp.sum(sq, axis=-1, keepdims=True)
    ms = jnp.where(first, ss_first, ss_all - ss_first) * (1.0 / SB_DH)
    y_ref[0] = (o * lax.rsqrt(ms + NORM_EPS) * nw_ref[...]).astype(BF16)


def _sb(q, k, v, norm_row, tri, pad):
    b, lp, _ = q.shape
    t = SB_TILE
    grid = (b, D_SB // LANES, lp // t)
    kern = functools.partial(_sb_kernel, pad=pad)
    full = pl.BlockSpec((1, lp, LANES), lambda bi, hp, i: (bi, 0, hp))
    tile = pl.BlockSpec((1, t, LANES), lambda bi, hp, i: (bi, i, hp))
    return pl.pallas_call(
        kern,
        grid=grid,
        in_specs=[tile, full, full, pl.BlockSpec((1, LANES), lambda bi, hp, i: (0, hp)), _const_spec((t, t))],
        out_specs=tile,
        out_shape=jax.ShapeDtypeStruct((b, lp, D_SB), BF16),
        scratch_shapes=[pltpu.VMEM((2, t, LANES), F32), pltpu.VMEM((2, t, LANES), F32)],
        compiler_params=pltpu.CompilerParams(dimension_semantics=("parallel", "parallel", "arbitrary"),
                                             vmem_limit_bytes=VMEM_LIMIT),
        name="sb",
    )(q, k, v, norm_row, tri)


def _out_proj_kernel(yg_ref, ys_ref, h_ref, wg_ref, ws_ref, post_ref, pre_ref, h1_ref, u2_ref):
    mix = _dot(yg_ref[...], wg_ref[...]) + _dot(ys_ref[...], ws_ref[...])
    ms = jnp.mean(mix * mix, axis=-1, keepdims=True)
    h1 = h_ref[...] + mix * lax.rsqrt(ms + NORM_EPS) * post_ref[...]
    h1_ref[...] = h1
    ms1 = jnp.mean(h1 * h1, axis=-1, keepdims=True)
    u2_ref[...] = (h1 * lax.rsqrt(ms1 + NORM_EPS) * pre_ref[...]).astype(BF16)


def _out_proj(yg, ys, hp, wg, ws, post_w, pre_w, tm):
    rows, d = hp.shape
    row_spec = lambda n: pl.BlockSpec((tm, n), lambda i: (i, 0))
    return pl.pallas_call(
        _out_proj_kernel,
        grid=(rows // tm,),
        in_specs=[row_spec(D_GDN), row_spec(D_SB), row_spec(d), _const_spec(wg.shape), _const_spec(ws.shape),
                  _const_spec((1, d)), _const_spec((1, d))],
        out_specs=[row_spec(d), row_spec(d)],
        out_shape=[jax.ShapeDtypeStruct((rows, d), F32), jax.ShapeDtypeStruct((rows, d), BF16)],
        compiler_params=pltpu.CompilerParams(dimension_semantics=("parallel",), vmem_limit_bytes=VMEM_LIMIT),
        name="out_proj",
    )(yg, ys, hp, wg, ws, post_w, pre_w)


def _ffn_kernel(u_ref, halo_ref, h1_ref, wup_ref, cw_ref, cb_ref, wd_ref, post_ref, out_ref, uext, acc,
                *, tm, tiles_per_batch, d_ff, fc, pad):
    i = pl.program_id(0)
    keep = jnp.where(i % tiles_per_batch == 0, 0.0, 1.0).astype(BF16)
    uext[0:HALO, :] = halo_ref[...] * keep
    uext[HALO:, :] = u_ref[...]
    ue = uext[...]

    def conv(hx, cols):
        y = hx * cw_ref[FFN_CONV - 1:FFN_CONV, cols]
        for sh in range(1, FFN_CONV):
            y = y + pltpu.roll(hx, sh, axis=0) * cw_ref[FFN_CONV - 1 - sh:FFN_CONV - sh, cols]
        return y[HALO:, :] + cb_ref[:, cols]

    for cidx in range(d_ff // fc):
        gcols = slice(cidx * fc, (cidx + 1) * fc)
        ucols = slice(d_ff + cidx * fc, d_ff + (cidx + 1) * fc)
        gate = conv(_dot(ue, wup_ref[:, gcols]), gcols)
        up = conv(_dot(ue, wup_ref[:, ucols]), ucols)
        act = (jax.nn.gelu(gate, approximate=True) * up).astype(BF16)
        part = _dot(act, wd_ref[gcols, :])
        if cidx == 0:
            acc[...] = part
        else:
            acc[...] += part

    f = acc[...]
    ms = jnp.mean(f * f, axis=-1, keepdims=True)
    out = h1_ref[...] + f * lax.rsqrt(ms + NORM_EPS) * post_ref[...]
    t = (i % tiles_per_batch) * tm + lax.broadcasted_iota(jnp.int32, out.shape, 0)
    out_ref[...] = jnp.where(t >= pad, out, 0.0)


def _ffn(u2, h1, w_up, conv_w, conv_b, w_down, post_w, tm, tiles_per_batch, fc, pad):
    rows, d = h1.shape
    d_ff = w_down.shape[0]
    kern = functools.partial(_ffn_kernel, tm=tm, tiles_per_batch=tiles_per_batch, d_ff=d_ff, fc=fc, pad=pad)
    row_spec = lambda n: pl.BlockSpec((tm, n), lambda i: (i, 0))
    halo_spec = pl.BlockSpec((HALO, d), lambda i: (jnp.maximum(i * (tm // HALO) - 1, 0), 0))
    return pl.pallas_call(
        kern,
        grid=(rows // tm,),
        in_specs=[row_spec(d), halo_spec, row_spec(d), _const_spec(w_up.shape), _const_spec(conv_w.shape),
                  _const_spec(conv_b.shape), _const_spec(w_down.shape), _const_spec((1, d))],
        out_specs=row_spec(d),
        out_shape=jax.ShapeDtypeStruct((rows, d), F32),
        scratch_shapes=[pltpu.VMEM((tm + HALO, d), BF16), pltpu.VMEM((tm, d), F32)],
        compiler_params=pltpu.CompilerParams(dimension_semantics=("parallel",), vmem_limit_bytes=VMEM_LIMIT),
        name="ffn",
    )(u2, u2, h1, w_up, conv_w, conv_b, w_down, post_w)


def _layer(hp, pad, attn_pre_norm, w_in, gdn_conv_w, gdn_A_log, gdn_dt_bias, gdn_norm_w, sb_norm_w, w_out,
           attn_post_norm, ffn_pre_norm, w_ffn_up, ffn_conv_w, ffn_conv_b, w_ffn_down, ffn_post_norm):
    b, lp, d = hp.shape
    rows = b * lp
    tm = _pick_tile(lp, (768, 512, 256))
    tg = _pick_tile(lp, (768, 512, 256))
    d_ff = w_ffn_down.shape[0]
    fc = _pick_tile(d_ff, (256, 128))

    o_a = 2 * GDN_QK + D_GDN
    o_z = o_a + 2 * GDN_HEADS
    w_main = jnp.concatenate([w_in[:, :o_a], w_in[:, o_z:]], axis=1).astype(BF16)
    w_ab = jnp.pad(w_in[:, o_a:o_z], ((0, 0), (0, LANES - 2 * GDN_HEADS))).astype(BF16)
    row = lambda v: v.reshape(1, -1).astype(F32)
    lane_row = lambda v: jnp.pad(v.astype(F32), (0, LANES - v.shape[0])).reshape(1, LANES)

    hp2 = hp.reshape(rows, d)
    qkvg, z, qs, ks, vs, ab = _in_proj(hp2, row(attn_pre_norm), w_main, w_ab, tm)
    r3 = lambda t: t.reshape(b, lp, t.shape[-1])
    yg = _gdn(r3(qkvg), r3(ab), r3(z), gdn_conv_w.astype(F32), lane_row(gdn_A_log), lane_row(gdn_dt_bias),
              row(gdn_norm_w), pad, tg)
    ti = lax.broadcasted_iota(jnp.int32, (SB_TILE, SB_TILE), 0)
    tj = lax.broadcasted_iota(jnp.int32, (SB_TILE, SB_TILE), 1)
    tri = (ti >= tj).astype(BF16)
    ys = _sb(r3(qs), r3(ks), r3(vs), row(sb_norm_w), tri, pad)
    h1, u2 = _out_proj(yg.reshape(rows, D_GDN), ys.reshape(rows, D_SB), hp2, w_out[:D_GDN].astype(BF16),
                       w_out[D_GDN:].astype(BF16), row(attn_post_norm), row(ffn_pre_norm), tm)
    out = _ffn(u2, h1, w_ffn_up.astype(BF16), ffn_conv_w.astype(F32), row(ffn_conv_b), w_ffn_down.astype(BF16),
               row(ffn_post_norm), tm, lp // tm, fc, pad)
    return out.reshape(b, lp, d)


def kernel(x, meta_tokens, attn_pre_norm, w_in, gdn_conv_w, gdn_A_log, gdn_dt_bias, gdn_norm_w, sb_norm_w, w_out,
           attn_post_norm, ffn_pre_norm, w_ffn_up, ffn_conv_w, ffn_conv_b, w_ffn_down, ffn_post_norm):
    b, seq, d = x.shape
    depth = w_in.shape[0]
    length = N_META + seq
    pad = (-length) % ROW_ALIGN
    meta = jnp.broadcast_to(meta_tokens[None].astype(x.dtype), (b, N_META, d))
    hp = jnp.concatenate([jnp.zeros((b, pad, d), x.dtype), meta, x], axis=1)
    for l in range(depth):
        hp = _layer(hp, pad, attn_pre_norm[l], w_in[l], gdn_conv_w[l], gdn_A_log[l], gdn_dt_bias[l], gdn_norm_w[l],
                    sb_norm_w[l], w_out[l], attn_post_norm[l], ffn_pre_norm[l], w_ffn_up[l], ffn_conv_w[l],
                    ffn_conv_b[l], w_ffn_down[l], ffn_post_norm[l])
    return hp[:, pad + N_META:]
```

```python
import functools
import math

import jax
import jax.numpy as jnp
from jax import lax
from jax.experimental import pallas as pl
from jax.experimental.pallas import tpu as pltpu

F32 = jnp.float32
BF16 = jnp.bfloat16

N_META = 16
GDN_HEADS = 4
GDN_DK = 128
GDN_DV = 128
GDN_CONV = 4
GDN_CHUNK = 64
GDN_QK = GDN_HEADS * GDN_DK
D_GDN = GDN_HEADS * GDN_DV
GDN_CONV_CH = 2 * GDN_QK + D_GDN
SB_HEADS = 8
SB_DH = 64
D_SB = SB_HEADS * SB_DH
FFN_CONV = 3
NORM_EPS = 1e-6
L2_EPS = 1e-6

LANES = 128
SB_TILE = 256
ROW_ALIGN = SB_TILE
HALO = 16
NEG_BIG = -1e30
LOG2E = 1.4426950408889634
VMEM_LIMIT = 56 * 1024 * 1024


def _pick_tile(n, candidates):
    for c in candidates:
        if n % c == 0:
            return c
    raise ValueError(f"no tile for {n}")


def _dot(a, b):
    return jnp.dot(a, b, preferred_element_type=F32)


def _dot_nt(a, b):
    return lax.dot_general(a, b, (((1,), (1,)), ((), ())), preferred_element_type=F32)


def _dot_tn(a, b):
    return lax.dot_general(a, b, (((0,), (0,)), ((), ())), preferred_element_type=F32)


def _split2(x):
    hi = x.astype(BF16)
    lo = (x - hi.astype(F32)).astype(BF16)
    return hi, lo


def _dot_split(a, b):
    ah, al = _split2(a)
    bh, bl = _split2(b)
    return _dot(ah, bh) + (_dot(ah, bl) + _dot(al, bh))


def _silu(x):
    return x * jax.nn.sigmoid(x)


def _softplus(x):
    return jnp.maximum(x, 0.0) + jnp.log(1.0 + jnp.exp(-jnp.abs(x)))


def _const_spec(shape):
    nd = len(shape)
    return pl.BlockSpec(shape, lambda *_: (0,) * nd, pipeline_mode=pl.Buffered(1))


def _in_proj_kernel(h_ref, nw_ref, w_ref, wab_ref, qkvg_ref, z_ref, qs_ref, ks_ref, vs_ref, ab_ref, u_scr):
    h = h_ref[...]
    ms = jnp.mean(h * h, axis=-1, keepdims=True)
    u_scr[...] = (h * lax.rsqrt(ms + NORM_EPS) * nw_ref[...]).astype(BF16)
    u = u_scr[...]
    c0 = 0
    qkvg_ref[...] = _dot(u, w_ref[:, c0:c0 + GDN_CONV_CH])
    c0 += GDN_CONV_CH
    z_ref[...] = _dot(u, w_ref[:, c0:c0 + D_GDN])
    c0 += D_GDN
    qs_ref[...] = (_dot(u, w_ref[:, c0:c0 + D_SB]) * (SB_DH ** -0.5)).astype(BF16)
    c0 += D_SB
    ks_ref[...] = _dot(u, w_ref[:, c0:c0 + D_SB]).astype(BF16)
    c0 += D_SB
    vs_ref[...] = _dot(u, w_ref[:, c0:c0 + D_SB]).astype(BF16)
    ab_ref[...] = _dot(u, wab_ref[...])


def _in_proj(hp, norm_w, w_main, w_ab, tm):
    rows, d = hp.shape
    grid = (rows // tm,)
    row_spec = lambda n: pl.BlockSpec((tm, n), lambda i: (i, 0))
    return pl.pallas_call(
        _in_proj_kernel,
        grid=grid,
        in_specs=[row_spec(d), _const_spec((1, d)), _const_spec(w_main.shape), _const_spec(w_ab.shape)],
        out_specs=[row_spec(GDN_CONV_CH), row_spec(D_GDN), row_spec(D_SB), row_spec(D_SB), row_spec(D_SB),
                   row_spec(LANES)],
        out_shape=[jax.ShapeDtypeStruct((rows, GDN_CONV_CH), F32), jax.ShapeDtypeStruct((rows, D_GDN), F32),
                   jax.ShapeDtypeStruct((rows, D_SB), BF16), jax.ShapeDtypeStruct((rows, D_SB), BF16),
                   jax.ShapeDtypeStruct((rows, D_SB), BF16), jax.ShapeDtypeStruct((rows, LANES), F32)],
        scratch_shapes=[pltpu.VMEM((tm, d), BF16)],
        compiler_params=pltpu.CompilerParams(dimension_semantics=("parallel",), vmem_limit_bytes=VMEM_LIMIT),
        name="in_proj",
    )(hp, norm_w, w_main, w_ab)


def _gdn_kernel(qkv_ref, ab_ref, z_ref, cw_ref, alog_ref, dtb_ref, nw_ref, y_ref,
                xbuf, qkvc, g_scr, beta_scr, o_scr, s_scr, *, pad, tg):
    i = pl.program_id(1)
    pair = 2 * GDN_CHUNK
    c = GDN_CHUNK

    @pl.when(i == 0)
    def _():
        s_scr[...] = jnp.zeros_like(s_scr)
        xbuf[0:HALO, :] = jnp.zeros((HALO, GDN_CONV_CH), F32)

    xbuf[HALO:HALO + tg, :] = qkv_ref[0]
    for grp in range(GDN_CONV_CH // LANES):
        cols = slice(grp * LANES, (grp + 1) * LANES)
        xe = xbuf[:, cols]
        acc = xe * cw_ref[GDN_CONV - 1:GDN_CONV, cols]
        for sh in range(1, GDN_CONV):
            acc = acc + pltpu.roll(xe, sh, axis=0) * cw_ref[GDN_CONV - 1 - sh:GDN_CONV - sh, cols]
        y = _silu(acc[HALO:, :])
        if grp < 2 * GDN_HEADS:
            y = y * lax.rsqrt(jnp.sum(y * y, axis=-1, keepdims=True) + L2_EPS)
            if grp < GDN_HEADS:
                y = y * (GDN_DK ** -0.5)
        qkvc[:, cols] = y
    xbuf[0:HALO, :] = xbuf[tg:tg + HALO, :]

    ab = ab_ref[0]
    rows = i * tg + lax.broadcasted_iota(jnp.int32, (tg, LANES), 0)
    valid = rows >= pad
    g = -jnp.exp(alog_ref[...]) * _softplus(ab + dtb_ref[...])
    g_scr[...] = jnp.where(valid, g, 0.0)
    beta_scr[...] = jnp.where(valid, jax.nn.sigmoid(pltpu.roll(ab, LANES - GDN_HEADS, axis=1)), 0.0)

    ri = lax.broadcasted_iota(jnp.int32, (pair, pair), 0)
    ci = lax.broadcasted_iota(jnp.int32, (pair, pair), 1)
    same_chunk = jnp.right_shift(ri, int(math.log2(c))) == jnp.right_shift(ci, int(math.log2(c)))
    incl = (ci <= ri) & same_chunk
    strict = (ci < ri) & same_chunk
    cum_mat = jnp.where(incl, 1.0, 0.0).astype(BF16)
    eye = jnp.where(ri == ci, 1.0, 0.0)
    upper_rows = lax.broadcasted_iota(jnp.int32, (pair, 1), 0) < c
    heads = range(GDN_HEADS)

    def pair_body(p, carry):
        r0 = pl.multiple_of(p * pair, pair)
        rows_p = pl.ds(r0, pair)
        gp = g_scr[rows_p, :]
        g_hi = gp.astype(BF16)
        g_r1 = gp - g_hi.astype(F32)
        g_mid = g_r1.astype(BF16)
        g_lo = (g_r1 - g_mid.astype(F32)).astype(BF16)
        gc = _dot(cum_mat, g_hi) + (_dot(cum_mat, g_mid) + _dot(cum_mat, g_lo))
        gct = gc.T
        bp = beta_scr[rows_p, :]
        g_col = [gc[:, hd:hd + 1] for hd in heads]
        g_last = [[gc[s * c + c - 1:s * c + c, hd:hd + 1] for s in range(2)] for hd in heads]
        b_col = [bp[:, hd:hd + 1] for hd in heads]
        qc = [qkvc[rows_p, hd * GDN_DK:(hd + 1) * GDN_DK] for hd in heads]
        kc = [qkvc[rows_p, GDN_QK + hd * GDN_DK:GDN_QK + (hd + 1) * GDN_DK] for hd in heads]
        vc = [qkvc[rows_p, 2 * GDN_QK + hd * GDN_DV:2 * GDN_QK + (hd + 1) * GDN_DV] for hd in heads]
        decay = [jnp.exp(jnp.where(incl, g_col[hd] - gct[hd:hd + 1, :], NEG_BIG)) for hd in heads]
        eg = [jnp.exp(g_col[hd]) for hd in heads]
        kb = [kc[hd] * b_col[hd] for hd in heads]
        kc16 = [kc[hd].astype(BF16) for hd in heads]
        a_mat = [jnp.where(strict, _dot_nt(kb[hd].astype(BF16), kc16[hd]) * decay[hd], 0.0) for hd in heads]
        pw = [-a for a in a_mat]
        t_mat = [eye + x for x in pw]
        for _ in range(int(math.log2(c)) - 1):
            pw = [_dot_split(x, x) for x in pw]
            t_mat = [tm + _dot_split(tm, x) for tm, x in zip(t_mat, pw)]
        t16 = [tm.astype(BF16) for tm in t_mat]
        u = [_dot(t16[hd], (vc[hd] * b_col[hd]).astype(BF16)) for hd in heads]
        w16 = [_dot(t16[hd], (kb[hd] * eg[hd]).astype(BF16)).astype(BF16) for hd in heads]
        qk16 = [(_dot_nt(qc[hd].astype(BF16), kc16[hd]) * decay[hd]).astype(BF16) for hd in heads]
        qd16 = [(qc[hd] * eg[hd]).astype(BF16) for hd in heads]
        kd16 = [(kc[hd] * jnp.exp(jnp.where(upper_rows, g_last[hd][0], g_last[hd][1]) - g_col[hd])).astype(BF16)
                for hd in heads]
        zeros_c = jnp.zeros((c, GDN_DV), BF16)
        for s in range(2):
            rs = slice(s * c, (s + 1) * c)
            s_old = [s_scr[hd] for hd in heads]
            s16 = [x.astype(BF16) for x in s_old]
            v16 = [(u[hd][rs] - _dot(w16[hd][rs], s16[hd])).astype(BF16) for hd in heads]
            vpad = [jnp.concatenate([v16[hd], zeros_c] if s == 0 else [zeros_c, v16[hd]], axis=0) for hd in heads]
            o = [_dot(qd16[hd][rs], s16[hd]) + _dot(qk16[hd][rs], vpad[hd]) for hd in heads]
            for hd in heads:
                s_scr[hd] = s_old[hd] * jnp.exp(g_last[hd][s]) + _dot_tn(kd16[hd][rs], v16[hd])
                o_scr[pl.ds(r0 + s * c, c), hd * GDN_DV:(hd + 1) * GDN_DV] = o[hd]
        return carry

    lax.fori_loop(0, tg // pair, pair_body, 0)

    for hd in range(GDN_HEADS):
        cols = slice(hd * GDN_DV, (hd + 1) * GDN_DV)
        o = o_scr[:, cols]
        ms = jnp.mean(o * o, axis=-1, keepdims=True)
        y = o * lax.rsqrt(ms + NORM_EPS) * nw_ref[...]
        y_ref[0, :, cols] = (y * _silu(z_ref[0, :, cols])).astype(BF16)


def _gdn(qkv, ab, z, conv_w, alog_row, dtb_row, norm_row, pad, tg):
    b, lp, _ = qkv.shape
    grid = (b, lp // tg)
    tile = lambda n: pl.BlockSpec((1, tg, n), lambda bi, i: (bi, i, 0))
    kern = functools.partial(_gdn_kernel, pad=pad, tg=tg)
    return pl.pallas_call(
        kern,
        grid=grid,
        in_specs=[tile(GDN_CONV_CH), tile(LANES), tile(D_GDN), _const_spec(conv_w.shape),
                  _const_spec((1, LANES)), _const_spec((1, LANES)), _const_spec((1, GDN_DV))],
        out_specs=tile(D_GDN),
        out_shape=jax.ShapeDtypeStruct((b, lp, D_GDN), BF16),
        scratch_shapes=[pltpu.VMEM((tg + HALO, GDN_CONV_CH), F32),
                        pltpu.VMEM((tg, GDN_CONV_CH), F32),
                        pltpu.VMEM((tg, LANES), F32),
                        pltpu.VMEM((tg, LANES), F32),
                        pltpu.VMEM((tg, D_GDN), F32),
                        pltpu.VMEM((GDN_HEADS, GDN_DK, GDN_DV), F32)],
        compiler_params=pltpu.CompilerParams(dimension_semantics=("arbitrary", "arbitrary"),
                                             vmem_limit_bytes=VMEM_LIMIT),
        name="gdn",
    )(qkv, ab, z, conv_w, alog_row, dtb_row, norm_row)


def _sb_kernel(q_ref, k_ref, v_ref, nw_ref, tri_ref, y_ref, qm_scr, acc_scr, run_scr, *, pad):
    i = pl.program_id(1)
    t = SB_TILE
    nh = SB_HEADS
    lane = lax.broadcasted_iota(jnp.int32, (t, LANES), 1)
    first = lane < SB_DH
    pair_cols = lambda hd: slice((hd // 2) * LANES, (hd // 2 + 1) * LANES)
    for hd in range(nh):
        qp = q_ref[0, :, pair_cols(hd)]
        zero = jnp.zeros_like(qp)
        qm_scr[hd] = jnp.where(first, qp, zero) if hd % 2 == 0 else jnp.where(first, zero, qp)
    tri = tri_ref[...]
    acc_scr[...] = jnp.zeros_like(acc_scr)
    run_scr[...] = jnp.zeros_like(run_scr)
    qi = lax.broadcasted_iota(jnp.int32, (t, t), 0)
    ki = lax.broadcasted_iota(jnp.int32, (t, t), 1)

    def block(j, mask):
        r0 = pl.multiple_of(j * t, t)
        z, hi, lo, rsum, cs, att = ({} for _ in range(6))
        for step in range(nh + 4):
            hd = step
            if hd < nh:
                z[hd] = _dot_nt(qm_scr[hd], k_ref[0, pl.ds(r0, t), pair_cols(hd)])
            hd = step - 1
            if 0 <= hd < nh:
                sp = jnp.maximum(z[hd], 0.0) + jnp.log(1.0 + jnp.exp2(jnp.abs(z[hd]) * (-LOG2E)))
                if mask is not None:
                    sp = jnp.where(mask, sp, 0.0)
                hi[hd], lo[hd] = _split2(sp)
                rsum[hd] = jnp.sum(sp, axis=1, keepdims=True)
            hd = step - 2
            if 0 <= hd < nh:
                cs[hd] = _dot(hi[hd], tri) + _dot(lo[hd], tri)
            hd = step - 3
            if 0 <= hd < nh:
                run = run_scr[hd]
                e = (z[hd] - cs[hd]) - jnp.concatenate([run, run], axis=1)
                if mask is not None:
                    e = jnp.where(mask, e, NEG_BIG)
                att[hd] = jnp.exp(e).astype(BF16)
                run_scr[hd] = run + rsum[hd]
            hd = step - 4
            if 0 <= hd < nh:
                acc_scr[hd] += _dot(att[hd], v_ref[0, pl.ds(r0, t), pair_cols(hd)])

    block(i, (ki < qi) & (ki + i * t >= pad))

    def interior(n, carry):
        block(i - 1 - n, None)
        return carry

    lax.fori_loop(0, jnp.maximum(i - 1, 0), interior, 0)

    @pl.when(i >= 1)
    def _():
        block(0, ki >= pad)

    for pr in range(nh // 2):
        o = jnp.where(first, acc_scr[2 * pr], acc_scr[2 * pr + 1])
        sq = o * o
        ss_first = jnp.sum(jnp.where(first, sq, 0.0), axis=-1, keepdims=True)
        ss_all = jnp.sum(sq, axis=-1, keepdims=True)
        ms = jnp.where(first, ss_first, ss_all - ss_first) * (1.0 / SB_DH)
        cols = slice(pr * LANES, (pr + 1) * LANES)
        y_ref[0, :, cols] = (o * lax.rsqrt(ms + NORM_EPS) * nw_ref[:, cols]).astype(BF16)


def _sb(q, k, v, norm_row, tri, pad):
    b, lp, _ = q.shape
    t = SB_TILE
    kern = functools.partial(_sb_kernel, pad=pad)
    full = pl.BlockSpec((1, lp, D_SB), lambda bi, i: (bi, 0, 0), pipeline_mode=pl.Buffered(1))
    tile = pl.BlockSpec((1, t, D_SB), lambda bi, i: (bi, i, 0))
    return pl.pallas_call(
        kern,
        grid=(b, lp // t),
        in_specs=[tile, full, full, _const_spec((1, D_SB)), _const_spec((t, t))],
        out_specs=tile,
        out_shape=jax.ShapeDtypeStruct((b, lp, D_SB), BF16),
        scratch_shapes=[pltpu.VMEM((SB_HEADS, t, LANES), BF16),
                        pltpu.VMEM((SB_HEADS, t, LANES), F32),
                        pltpu.VMEM((SB_HEADS, t, LANES), F32)],
        compiler_params=pltpu.CompilerParams(dimension_semantics=("parallel", "arbitrary"),
                                             vmem_limit_bytes=VMEM_LIMIT),
        name="sb",
    )(q, k, v, norm_row, tri)


def _out_proj_kernel(yg_ref, ys_ref, h_ref, wg_ref, ws_ref, post_ref, pre_ref, h1_ref, u2_ref):
    mix = _dot(yg_ref[...], wg_ref[...]) + _dot(ys_ref[...], ws_ref[...])
    ms = jnp.mean(mix * mix, axis=-1, keepdims=True)
    h1 = h_ref[...] + mix * lax.rsqrt(ms + NORM_EPS) * post_ref[...]
    h1_ref[...] = h1
    ms1 = jnp.mean(h1 * h1, axis=-1, keepdims=True)
    u2_ref[...] = (h1 * lax.rsqrt(ms1 + NORM_EPS) * pre_ref[...]).astype(BF16)


def _out_proj(yg, ys, hp, wg, ws, post_w, pre_w, tm):
    rows, d = hp.shape
    row_spec = lambda n: pl.BlockSpec((tm, n), lambda i: (i, 0))
    return pl.pallas_call(
        _out_proj_kernel,
        grid=(rows // tm,),
        in_specs=[row_spec(D_GDN), row_spec(D_SB), row_spec(d), _const_spec(wg.shape), _const_spec(ws.shape),
                  _const_spec((1, d)), _const_spec((1, d))],
        out_specs=[row_spec(d), row_spec(d)],
        out_shape=[jax.ShapeDtypeStruct((rows, d), F32), jax.ShapeDtypeStruct((rows, d), BF16)],
        compiler_params=pltpu.CompilerParams(dimension_semantics=("parallel",), vmem_limit_bytes=VMEM_LIMIT),
        name="out_proj",
    )(yg, ys, hp, wg, ws, post_w, pre_w)


def _ffn_kernel(u_ref, halo_ref, h1_ref, wup_ref, cw_ref, cb_ref, wd_ref, post_ref, out_ref, uext, acc,
                *, tm, tiles_per_batch, d_ff, fc, pad):
    i = pl.program_id(0)
    keep = jnp.where(i % tiles_per_batch == 0, 0.0, 1.0).astype(BF16)
    uext[0:HALO, :] = halo_ref[...] * keep
    uext[HALO:, :] = u_ref[...]
    ue = uext[...]

    def conv(hx, cols):
        y = hx * cw_ref[FFN_CONV - 1:FFN_CONV, cols]
        for sh in range(1, FFN_CONV):
            y = y + pltpu.roll(hx, sh, axis=0) * cw_ref[FFN_CONV - 1 - sh:FFN_CONV - sh, cols]
        return y[HALO:, :] + cb_ref[:, cols]

    for cidx in range(d_ff // fc):
        gcols = slice(cidx * fc, (cidx + 1) * fc)
        ucols = slice(d_ff + cidx * fc, d_ff + (cidx + 1) * fc)
        gate = conv(_dot(ue, wup_ref[:, gcols]), gcols)
        up = conv(_dot(ue, wup_ref[:, ucols]), ucols)
        act = (jax.nn.gelu(gate, approximate=True) * up).astype(BF16)
        part = _dot(act, wd_ref[gcols, :])
        if cidx == 0:
            acc[...] = part
        else:
            acc[...] += part

    f = acc[...]
    ms = jnp.mean(f * f, axis=-1, keepdims=True)
    out = h1_ref[...] + f * lax.rsqrt(ms + NORM_EPS) * post_ref[...]
    t = (i % tiles_per_batch) * tm + lax.broadcasted_iota(jnp.int32, out.shape, 0)
    out_ref[...] = jnp.where(t >= pad, out, 0.0)


def _ffn(u2, h1, w_up, conv_w, conv_b, w_down, post_w, tm, tiles_per_batch, fc, pad):
    rows, d = h1.shape
    d_ff = w_down.shape[0]
    kern = functools.partial(_ffn_kernel, tm=tm, tiles_per_batch=tiles_per_batch, d_ff=d_ff, fc=fc, pad=pad)
    row_spec = lambda n: pl.BlockSpec((tm, n), lambda i: (i, 0))
    halo_spec = pl.BlockSpec((HALO, d), lambda i: (jnp.maximum(i * (tm // HALO) - 1, 0), 0))
    return pl.pallas_call(
        kern,
        grid=(rows // tm,),
        in_specs=[row_spec(d), halo_spec, row_spec(d), _const_spec(w_up.shape), _const_spec(conv_w.shape),
                  _const_spec(conv_b.shape), _const_spec(w_down.shape), _const_spec((1, d))],
        out_specs=row_spec(d),
        out_shape=jax.ShapeDtypeStruct((rows, d), F32),
        scratch_shapes=[pltpu.VMEM((tm + HALO, d), BF16), pltpu.VMEM((tm, d), F32)],
        compiler_params=pltpu.CompilerParams(dimension_semantics=("parallel",), vmem_limit_bytes=VMEM_LIMIT),
        name="ffn",
    )(u2, u2, h1, w_up, conv_w, conv_b, w_down, post_w)


def _layer(hp, pad, attn_pre_norm, w_in, gdn_conv_w, gdn_A_log, gdn_dt_bias, gdn_norm_w, sb_norm_w, w_out,
           attn_post_norm, ffn_pre_norm, w_ffn_up, ffn_conv_w, ffn_conv_b, w_ffn_down, ffn_post_norm):
    b, lp, d = hp.shape
    rows = b * lp
    tm = _pick_tile(lp, (768, 512, 256))
    tg = _pick_tile(lp, (768, 512, 256))
    d_ff = w_ffn_down.shape[0]
    fc = _pick_tile(d_ff, (256, 128))

    o_a = 2 * GDN_QK + D_GDN
    o_z = o_a + 2 * GDN_HEADS
    w_main = jnp.concatenate([w_in[:, :o_a], w_in[:, o_z:]], axis=1).astype(BF16)
    w_ab = jnp.pad(w_in[:, o_a:o_z], ((0, 0), (0, LANES - 2 * GDN_HEADS))).astype(BF16)
    row = lambda v: v.reshape(1, -1).astype(F32)
    lane_row = lambda v: jnp.pad(v.astype(F32), (0, LANES - v.shape[0])).reshape(1, LANES)

    hp2 = hp.reshape(rows, d)
    qkvg, z, qs, ks, vs, ab = _in_proj(hp2, row(attn_pre_norm), w_main, w_ab, tm)
    r3 = lambda t: t.reshape(b, lp, t.shape[-1])
    yg = _gdn(r3(qkvg), r3(ab), r3(z), gdn_conv_w.astype(F32), lane_row(gdn_A_log), lane_row(gdn_dt_bias),
              row(gdn_norm_w), pad, tg)
    ti = lax.broadcasted_iota(jnp.int32, (SB_TILE, SB_TILE), 0)
    tj = lax.broadcasted_iota(jnp.int32, (SB_TILE, SB_TILE), 1)
    tri = (ti >= tj).astype(BF16)
    ys = _sb(r3(qs), r3(ks), r3(vs), row(sb_norm_w), tri, pad)
    h1, u2 = _out_proj(yg.reshape(rows, D_GDN), ys.reshape(rows, D_SB), hp2, w_out[:D_GDN].astype(BF16),
                       w_out[D_GDN:].astype(BF16), row(attn_post_norm), row(ffn_pre_norm), tm)
    out = _ffn(u2, h1, w_ffn_up.astype(BF16), ffn_conv_w.astype(F32), row(ffn_conv_b), w_ffn_down.astype(BF16),
               row(ffn_post_norm), tm, lp // tm, fc, pad)
    return out.reshape(b, lp, d)


def kernel(x, meta_tokens, attn_pre_norm, w_in, gdn_conv_w, gdn_A_log, gdn_dt_bias, gdn_norm_w, sb_norm_w, w_out,
           attn_post_norm, ffn_pre_norm, w_ffn_up, ffn_conv_w, ffn_conv_b, w_ffn_down, ffn_post_norm):
    b, seq, d = x.shape
    depth = w_in.shape[0]
    length = N_META + seq
    pad = (-length) % ROW_ALIGN
    meta = jnp.broadcast_to(meta_tokens[None].astype(x.dtype), (b, N_META, d))
    hp = jnp.concatenate([jnp.zeros((b, pad, d), x.dtype), meta, x], axis=1)
    for l in range(depth):
        hp = _layer(hp, pad, attn_pre_norm[l], w_in[l], gdn_conv_w[l], gdn_A_log[l], gdn_dt_bias[l], gdn_norm_w[l],
                    sb_norm_w[l], w_out[l], attn_post_norm[l], ffn_pre_norm[l], w_ffn_up[l], ffn_conv_w[l],
                    ffn_conv_b[l], w_ffn_down[l], ffn_post_norm[l])
    return hp[:, pad + N_META:]
```

```python
import functools
import itertools
import math

import jax
import jax.numpy as jnp
from jax import lax
from jax.experimental import pallas as pl
from jax.experimental.pallas import tpu as pltpu

F32 = jnp.float32
BF16 = jnp.bfloat16

N_META = 16
GDN_HEADS = 4
GDN_DK = 128
GDN_DV = 128
GDN_CONV = 4
GDN_CHUNK = 64
GDN_QK = GDN_HEADS * GDN_DK
D_GDN = GDN_HEADS * GDN_DV
GDN_CONV_CH = 2 * GDN_QK + D_GDN
SB_HEADS = 8
SB_DH = 64
D_SB = SB_HEADS * SB_DH
FFN_CONV = 3
NORM_EPS = 1e-6
L2_EPS = 1e-6

LANES = 128
SB_TILE = 256
ROW_ALIGN = SB_TILE
HALO = 16
NEG_BIG = -1e30
LOG2E = 1.4426950408889634
VMEM_LIMIT = 56 * 1024 * 1024


def _pick_tile(n, candidates):
    for c in candidates:
        if n % c == 0:
            return c
    raise ValueError(f"no tile for {n}")


def _dot(a, b):
    return jnp.dot(a, b, preferred_element_type=F32)


def _dot_nt(a, b):
    return lax.dot_general(a, b, (((1,), (1,)), ((), ())), preferred_element_type=F32)


def _dot_tn(a, b):
    return lax.dot_general(a, b, (((0,), (0,)), ((), ())), preferred_element_type=F32)


def _split2(x):
    hi = x.astype(BF16)
    lo = (x - hi.astype(F32)).astype(BF16)
    return hi, lo


def _dot_split(a, b):
    ah, al = _split2(a)
    bh, bl = _split2(b)
    return _dot(ah, bh) + (_dot(ah, bl) + _dot(al, bh))


def _silu(x):
    return x * jax.nn.sigmoid(x)


def _softplus(x):
    return jnp.maximum(x, 0.0) + jnp.log(1.0 + jnp.exp(-jnp.abs(x)))


def _const_spec(shape):
    nd = len(shape)
    return pl.BlockSpec(shape, lambda *_: (0,) * nd, pipeline_mode=pl.Buffered(1))


def _in_proj_kernel(h_ref, nw_ref, w_ref, wab_ref, qkvg_ref, z_ref, qs_ref, ks_ref, vs_ref, ab_ref, u_scr):
    h = h_ref[...]
    ms = jnp.mean(h * h, axis=-1, keepdims=True)
    u_scr[...] = (h * lax.rsqrt(ms + NORM_EPS) * nw_ref[...]).astype(BF16)
    u = u_scr[...]
    c0 = 0
    qkvg_ref[...] = _dot(u, w_ref[:, c0:c0 + GDN_CONV_CH])
    c0 += GDN_CONV_CH
    z_ref[...] = _dot(u, w_ref[:, c0:c0 + D_GDN])
    c0 += D_GDN
    qs_ref[...] = (_dot(u, w_ref[:, c0:c0 + D_SB]) * (SB_DH ** -0.5)).astype(BF16)
    c0 += D_SB
    ks_ref[...] = _dot(u, w_ref[:, c0:c0 + D_SB]).astype(BF16)
    c0 += D_SB
    vs_ref[...] = _dot(u, w_ref[:, c0:c0 + D_SB]).astype(BF16)
    ab_ref[...] = _dot(u, wab_ref[...])


def _in_proj(hp, norm_w, w_main, w_ab, tm):
    rows, d = hp.shape
    grid = (rows // tm,)
    row_spec = lambda n: pl.BlockSpec((tm, n), lambda i: (i, 0))
    return pl.pallas_call(
        _in_proj_kernel,
        grid=grid,
        in_specs=[row_spec(d), _const_spec((1, d)), _const_spec(w_main.shape), _const_spec(w_ab.shape)],
        out_specs=[row_spec(GDN_CONV_CH), row_spec(D_GDN), row_spec(D_SB), row_spec(D_SB), row_spec(D_SB),
                   row_spec(LANES)],
        out_shape=[jax.ShapeDtypeStruct((rows, GDN_CONV_CH), F32), jax.ShapeDtypeStruct((rows, D_GDN), F32),
                   jax.ShapeDtypeStruct((rows, D_SB), BF16), jax.ShapeDtypeStruct((rows, D_SB), BF16),
                   jax.ShapeDtypeStruct((rows, D_SB), BF16), jax.ShapeDtypeStruct((rows, LANES), F32)],
        scratch_shapes=[pltpu.VMEM((tm, d), BF16)],
        compiler_params=pltpu.CompilerParams(dimension_semantics=("parallel",), vmem_limit_bytes=VMEM_LIMIT),
        name="in_proj",
    )(hp, norm_w, w_main, w_ab)


def _gdn_kernel(qkv_ref, ab_ref, z_ref, cw_ref, alog_ref, dtb_ref, nw_ref, y_ref,
                xbuf, qkvc, g_scr, beta_scr, o_scr, s_scr, pu_scr, pw_scr, pqk_scr, pqd_scr, pkd_scr, pgl_scr,
                *, pad, tg):
    i = pl.program_id(1)
    pair = 2 * GDN_CHUNK
    c = GDN_CHUNK

    @pl.when(i == 0)
    def _():
        s_scr[...] = jnp.zeros_like(s_scr)
        xbuf[0:HALO, :] = jnp.zeros((HALO, GDN_CONV_CH), F32)

    xbuf[HALO:HALO + tg, :] = qkv_ref[0]
    for grp in range(GDN_CONV_CH // LANES):
        cols = slice(grp * LANES, (grp + 1) * LANES)
        xe = xbuf[:, cols]
        acc = xe * cw_ref[GDN_CONV - 1:GDN_CONV, cols]
        for sh in range(1, GDN_CONV):
            acc = acc + pltpu.roll(xe, sh, axis=0) * cw_ref[GDN_CONV - 1 - sh:GDN_CONV - sh, cols]
        y = _silu(acc[HALO:, :])
        if grp < 2 * GDN_HEADS:
            y = y * lax.rsqrt(jnp.sum(y * y, axis=-1, keepdims=True) + L2_EPS)
            if grp < GDN_HEADS:
                y = y * (GDN_DK ** -0.5)
        qkvc[:, cols] = y
    xbuf[0:HALO, :] = xbuf[tg:tg + HALO, :]

    ab = ab_ref[0]
    rows = i * tg + lax.broadcasted_iota(jnp.int32, (tg, LANES), 0)
    valid = rows >= pad
    g = -jnp.exp(alog_ref[...]) * _softplus(ab + dtb_ref[...])
    g_scr[...] = jnp.where(valid, g, 0.0)
    beta_scr[...] = jnp.where(valid, jax.nn.sigmoid(pltpu.roll(ab, LANES - GDN_HEADS, axis=1)), 0.0)

    ri = lax.broadcasted_iota(jnp.int32, (pair, pair), 0)
    ci = lax.broadcasted_iota(jnp.int32, (pair, pair), 1)
    same_chunk = jnp.right_shift(ri, int(math.log2(c))) == jnp.right_shift(ci, int(math.log2(c)))
    incl = (ci <= ri) & same_chunk
    strict = (ci < ri) & same_chunk
    cum_mat = jnp.where(incl, 1.0, 0.0).astype(BF16)
    eye = jnp.where(ri == ci, 1.0, 0.0)
    upper_rows = lax.broadcasted_iota(jnp.int32, (pair, 1), 0) < c
    heads = range(GDN_HEADS)

    n_pairs = tg // pair

    def prepare(p, slot):
        r0 = pl.multiple_of(p * pair, pair)
        rows_p = pl.ds(r0, pair)
        gp = g_scr[rows_p, :]
        g_hi = gp.astype(BF16)
        g_r1 = gp - g_hi.astype(F32)
        g_mid = g_r1.astype(BF16)
        g_lo = (g_r1 - g_mid.astype(F32)).astype(BF16)
        gc = _dot(cum_mat, g_hi) + (_dot(cum_mat, g_mid) + _dot(cum_mat, g_lo))
        gct = gc.T
        bp = beta_scr[rows_p, :]
        g_col = [gc[:, hd:hd + 1] for hd in heads]
        g_last = [[gc[s * c + c - 1:s * c + c, hd:hd + 1] for s in range(2)] for hd in heads]
        b_col = [bp[:, hd:hd + 1] for hd in heads]
        qc = [qkvc[rows_p, hd * GDN_DK:(hd + 1) * GDN_DK] for hd in heads]
        kc = [qkvc[rows_p, GDN_QK + hd * GDN_DK:GDN_QK + (hd + 1) * GDN_DK] for hd in heads]
        vc = [qkvc[rows_p, 2 * GDN_QK + hd * GDN_DV:2 * GDN_QK + (hd + 1) * GDN_DV] for hd in heads]
        decay = [jnp.exp(jnp.where(incl, g_col[hd] - gct[hd:hd + 1, :], NEG_BIG)) for hd in heads]
        eg = [jnp.exp(g_col[hd]) for hd in heads]
        kb = [kc[hd] * b_col[hd] for hd in heads]
        kc16 = [kc[hd].astype(BF16) for hd in heads]
        a_mat = [jnp.where(strict, _dot_nt(kb[hd].astype(BF16), kc16[hd]) * decay[hd], 0.0) for hd in heads]
        pw = [-a for a in a_mat]
        t_mat = [eye + x for x in pw]
        yield
        for _ in range(int(math.log2(c)) - 1):
            pw16 = [x.astype(BF16) for x in pw]
            pw = [_dot(x, x) for x in pw16]
            t_mat = [tm + _dot(tm.astype(BF16), x.astype(BF16)) for tm, x in zip(t_mat, pw)]
            yield
        t16 = [tm.astype(BF16) for tm in t_mat]
        for hd in heads:
            pu_scr[slot, hd] = _dot(t16[hd], (vc[hd] * b_col[hd]).astype(BF16))
            pw_scr[slot, hd] = _dot(t16[hd], (kb[hd] * eg[hd]).astype(BF16)).astype(BF16)
            pqk_scr[slot, hd] = (_dot_nt(qc[hd].astype(BF16), kc16[hd]) * decay[hd]).astype(BF16)
            pqd_scr[slot, hd] = (qc[hd] * eg[hd]).astype(BF16)
            pkd_scr[slot, hd] = (kc[hd] * jnp.exp(jnp.where(upper_rows, g_last[hd][0], g_last[hd][1])
                                                  - g_col[hd])).astype(BF16)
            for s in range(2):
                pgl_scr[slot, hd, s] = jnp.broadcast_to(jnp.exp(g_last[hd][s]), (8, LANES))
        yield

    def advance(p, slot):
        r0 = pl.multiple_of(p * pair, pair)
        zeros_c = jnp.zeros((c, GDN_DV), BF16)
        for s in range(2):
            rs = slice(s * c, (s + 1) * c)
            s_old = [s_scr[hd] for hd in heads]
            s16 = [x.astype(BF16) for x in s_old]
            v16 = [(pu_scr[slot, hd, rs, :] - _dot(pw_scr[slot, hd, rs, :], s16[hd])).astype(BF16) for hd in heads]
            yield
            vpad = [jnp.concatenate([v16[hd], zeros_c] if s == 0 else [zeros_c, v16[hd]], axis=0) for hd in heads]
            o = [_dot(pqd_scr[slot, hd, rs, :], s16[hd]) + _dot(pqk_scr[slot, hd, rs, :], vpad[hd]) for hd in heads]
            for hd in heads:
                s_scr[hd] = s_old[hd] * pgl_scr[slot, hd, s, 0:1, :] + _dot_tn(pkd_scr[slot, hd, rs, :], v16[hd])
                o_scr[pl.ds(r0 + s * c, c), hd * GDN_DV:(hd + 1) * GDN_DV] = o[hd]
            yield

    for _ in prepare(0, 0):
        pass

    def pair_body(p, carry):
        nxt = jnp.minimum(p + 1, n_pairs - 1)
        for _ in itertools.zip_longest(prepare(nxt, (p + 1) % 2), advance(p, p % 2)):
            pass
        return carry

    lax.fori_loop(0, n_pairs, pair_body, 0)

    for hd in range(GDN_HEADS):
        cols = slice(hd * GDN_DV, (hd + 1) * GDN_DV)
        o = o_scr[:, cols]
        ms = jnp.mean(o * o, axis=-1, keepdims=True)
        y = o * lax.rsqrt(ms + NORM_EPS) * nw_ref[...]
        y_ref[0, :, cols] = (y * _silu(z_ref[0, :, cols])).astype(BF16)


def _gdn(qkv, ab, z, conv_w, alog_row, dtb_row, norm_row, pad, tg):
    b, lp, _ = qkv.shape
    grid = (b, lp // tg)
    tile = lambda n: pl.BlockSpec((1, tg, n), lambda bi, i: (bi, i, 0))
    kern = functools.partial(_gdn_kernel, pad=pad, tg=tg)
    return pl.pallas_call(
        kern,
        grid=grid,
        in_specs=[tile(GDN_CONV_CH), tile(LANES), tile(D_GDN), _const_spec(conv_w.shape),
                  _const_spec((1, LANES)), _const_spec((1, LANES)), _const_spec((1, GDN_DV))],
        out_specs=tile(D_GDN),
        out_shape=jax.ShapeDtypeStruct((b, lp, D_GDN), BF16),
        scratch_shapes=[pltpu.VMEM((tg + HALO, GDN_CONV_CH), F32),
                        pltpu.VMEM((tg, GDN_CONV_CH), F32),
                        pltpu.VMEM((tg, LANES), F32),
                        pltpu.VMEM((tg, LANES), F32),
                        pltpu.VMEM((tg, D_GDN), F32),
                        pltpu.VMEM((GDN_HEADS, GDN_DK, GDN_DV), F32),
                        pltpu.VMEM((2, GDN_HEADS, 2 * GDN_CHUNK, GDN_DV), F32),
                        pltpu.VMEM((2, GDN_HEADS, 2 * GDN_CHUNK, GDN_DK), BF16),
                        pltpu.VMEM((2, GDN_HEADS, 2 * GDN_CHUNK, 2 * GDN_CHUNK), BF16),
                        pltpu.VMEM((2, GDN_HEADS, 2 * GDN_CHUNK, GDN_DK), BF16),
                        pltpu.VMEM((2, GDN_HEADS, 2 * GDN_CHUNK, GDN_DK), BF16),
                        pltpu.VMEM((2, GDN_HEADS, 2, 8, LANES), F32)],
        compiler_params=pltpu.CompilerParams(dimension_semantics=("arbitrary", "arbitrary"),
                                             vmem_limit_bytes=VMEM_LIMIT),
        name="gdn",
    )(qkv, ab, z, conv_w, alog_row, dtb_row, norm_row)


def _sb_kernel(q_ref, k_ref, v_ref, nw_ref, tri_ref, y_ref, qm_scr, acc_scr, run_scr, *, pad):
    i = pl.program_id(1)
    t = SB_TILE
    nh = SB_HEADS
    lane = lax.broadcasted_iota(jnp.int32, (t, LANES), 1)
    first = lane < SB_DH
    pair_cols = lambda hd: slice((hd // 2) * LANES, (hd // 2 + 1) * LANES)
    for hd in range(nh):
        qp = q_ref[0, :, pair_cols(hd)]
        zero = jnp.zeros_like(qp)
        qm_scr[hd] = jnp.where(first, qp, zero) if hd % 2 == 0 else jnp.where(first, zero, qp)
    tri = tri_ref[...]
    acc_scr[...] = jnp.zeros_like(acc_scr)
    run_scr[...] = jnp.zeros_like(run_scr)
    qi = lax.broadcasted_iota(jnp.int32, (t, t), 0)
    ki = lax.broadcasted_iota(jnp.int32, (t, t), 1)

    def blocks(js, mask):
        items = [(pl.ds(pl.multiple_of(j * t, t), t), hd) for j in js for hd in range(nh)]
        n = len(items)
        z, sp16, cs, att = ({} for _ in range(4))
        for step in range(n + 4):
            it = step
            if it < n:
                rows, hd = items[it]
                z[it] = _dot_nt(qm_scr[hd], k_ref[0, rows, pair_cols(hd)])
            it = step - 1
            if 0 <= it < n:
                zb = z[it].astype(BF16)
                sp = jnp.maximum(zb, 0.0) + jnp.log(1.0 + jnp.exp2(jnp.abs(zb) * (-LOG2E)))
                if mask is not None:
                    sp = jnp.where(mask, sp, jnp.zeros_like(sp))
                sp16[it] = sp
            it = step - 2
            if 0 <= it < n:
                cs[it] = _dot(sp16[it], tri)
            it = step - 3
            if 0 <= it < n:
                hd = items[it][1]
                run = run_scr[hd]
                e = (z[it] - cs[it]) - jnp.concatenate([run, run], axis=1)
                if mask is not None:
                    e = jnp.where(mask, e, NEG_BIG)
                att[it] = jnp.exp(e).astype(BF16)
                run_scr[hd] = run + cs[it][:, 0:1]
            it = step - 4
            if 0 <= it < n:
                rows, hd = items[it]
                acc_scr[hd] += _dot(att[it], v_ref[0, rows, pair_cols(hd)])

    blocks([i], (ki < qi) & (ki + i * t >= pad))

    n_inner = jnp.maximum(i - 1, 0)

    def inner_pair(n, carry):
        blocks([i - 1 - 2 * n, i - 2 - 2 * n], None)
        return carry

    lax.fori_loop(0, n_inner // 2, inner_pair, 0)

    @pl.when(n_inner % 2 == 1)
    def _():
        blocks([1], None)

    @pl.when(i >= 1)
    def _():
        blocks([0], ki >= pad)

    for pr in range(nh // 2):
        o = jnp.where(first, acc_scr[2 * pr], acc_scr[2 * pr + 1])
        sq = o * o
        ss_first = jnp.sum(jnp.where(first, sq, 0.0), axis=-1, keepdims=True)
        ss_all = jnp.sum(sq, axis=-1, keepdims=True)
        ms = jnp.where(first, ss_first, ss_all - ss_first) * (1.0 / SB_DH)
        cols = slice(pr * LANES, (pr + 1) * LANES)
        y_ref[0, :, cols] = (o * lax.rsqrt(ms + NORM_EPS) * nw_ref[:, cols]).astype(BF16)


def _sb(q, k, v, norm_row, tri, pad):
    b, lp, _ = q.shape
    t = SB_TILE
    kern = functools.partial(_sb_kernel, pad=pad)
    full = pl.BlockSpec((1, lp, D_SB), lambda bi, i: (bi, 0, 0), pipeline_mode=pl.Buffered(1))
    tile = pl.BlockSpec((1, t, D_SB), lambda bi, i: (bi, i, 0))
    return pl.pallas_call(
        kern,
        grid=(b, lp // t),
        in_specs=[tile, full, full, _const_spec((1, D_SB)), _const_spec((t, t))],
        out_specs=tile,
        out_shape=jax.ShapeDtypeStruct((b, lp, D_SB), BF16),
        scratch_shapes=[pltpu.VMEM((SB_HEADS, t, LANES), BF16),
                        pltpu.VMEM((SB_HEADS, t, LANES), F32),
                        pltpu.VMEM((SB_HEADS, t, LANES), F32)],
        compiler_params=pltpu.CompilerParams(dimension_semantics=("parallel", "arbitrary"),
                                             vmem_limit_bytes=VMEM_LIMIT),
        name="sb",
    )(q, k, v, norm_row, tri)


def _out_proj_kernel(yg_ref, ys_ref, h_ref, wg_ref, ws_ref, post_ref, pre_ref, h1_ref, u2_ref):
    mix = _dot(yg_ref[...], wg_ref[...]) + _dot(ys_ref[...], ws_ref[...])
    ms = jnp.mean(mix * mix, axis=-1, keepdims=True)
    h1 = h_ref[...] + mix * lax.rsqrt(ms + NORM_EPS) * post_ref[...]
    h1_ref[...] = h1
    ms1 = jnp.mean(h1 * h1, axis=-1, keepdims=True)
    u2_ref[...] = (h1 * lax.rsqrt(ms1 + NORM_EPS) * pre_ref[...]).astype(BF16)


def _out_proj(yg, ys, hp, wg, ws, post_w, pre_w, tm):
    rows, d = hp.shape
    row_spec = lambda n: pl.BlockSpec((tm, n), lambda i: (i, 0))
    return pl.pallas_call(
        _out_proj_kernel,
        grid=(rows // tm,),
        in_specs=[row_spec(D_GDN), row_spec(D_SB), row_spec(d), _const_spec(wg.shape), _const_spec(ws.shape),
                  _const_spec((1, d)), _const_spec((1, d))],
        out_specs=[row_spec(d), row_spec(d)],
        out_shape=[jax.ShapeDtypeStruct((rows, d), F32), jax.ShapeDtypeStruct((rows, d), BF16)],
        compiler_params=pltpu.CompilerParams(dimension_semantics=("parallel",), vmem_limit_bytes=VMEM_LIMIT),
        name="out_proj",
    )(yg, ys, hp, wg, ws, post_w, pre_w)


def _ffn_kernel(u_ref, halo_ref, h1_ref, wup_ref, cw_ref, cb_ref, wd_ref, post_ref, out_ref, uext, acc,
                *, tm, tiles_per_batch, d_ff, fc, pad):
    i = pl.program_id(0)
    keep = jnp.where(i % tiles_per_batch == 0, 0.0, 1.0).astype(BF16)
    uext[0:HALO, :] = halo_ref[...] * keep
    uext[HALO:, :] = u_ref[...]
    ue = uext[...]

    def conv(hx, cols):
        y = hx * cw_ref[FFN_CONV - 1:FFN_CONV, cols]
        for sh in range(1, FFN_CONV):
            y = y + pltpu.roll(hx, sh, axis=0) * cw_ref[FFN_CONV - 1 - sh:FFN_CONV - sh, cols]
        return y[HALO:, :] + cb_ref[:, cols]

    n_chunks = d_ff // fc
    gcols = [slice(c * fc, (c + 1) * fc) for c in range(n_chunks)]
    ucols = [slice(d_ff + c * fc, d_ff + (c + 1) * fc) for c in range(n_chunks)]
    hg, hu, act = {}, {}, {}
    for step in range(n_chunks + 2):
        c = step
        if c < n_chunks:
            hg[c] = _dot(ue, wup_ref[:, gcols[c]])
            hu[c] = _dot(ue, wup_ref[:, ucols[c]])
        c = step - 1
        if 0 <= c < n_chunks:
            gate = conv(hg.pop(c), gcols[c])
            up = conv(hu.pop(c), ucols[c])
            act[c] = (jax.nn.gelu(gate, approximate=True) * up).astype(BF16)
        c = step - 2
        if 0 <= c < n_chunks:
            part = _dot(act.pop(c), wd_ref[gcols[c], :])
            if c == 0:
                acc[...] = part
            else:
                acc[...] += part

    f = acc[...]
    ms = jnp.mean(f * f, axis=-1, keepdims=True)
    out = h1_ref[...] + f * lax.rsqrt(ms + NORM_EPS) * post_ref[...]
    t = (i % tiles_per_batch) * tm + lax.broadcasted_iota(jnp.int32, out.shape, 0)
    out_ref[...] = jnp.where(t >= pad, out, 0.0)


def _ffn(u2, h1, w_up, conv_w, conv_b, w_down, post_w, tm, tiles_per_batch, fc, pad):
    rows, d = h1.shape
    d_ff = w_down.shape[0]
    kern = functools.partial(_ffn_kernel, tm=tm, tiles_per_batch=tiles_per_batch, d_ff=d_ff, fc=fc, pad=pad)
    row_spec = lambda n: pl.BlockSpec((tm, n), lambda i: (i, 0))
    halo_spec = pl.BlockSpec((HALO, d), lambda i: (jnp.maximum(i * (tm // HALO) - 1, 0), 0))
    return pl.pallas_call(
        kern,
        grid=(rows // tm,),
        in_specs=[row_spec(d), halo_spec, row_spec(d), _const_spec(w_up.shape), _const_spec(conv_w.shape),
                  _const_spec(conv_b.shape), _const_spec(w_down.shape), _const_spec((1, d))],
        out_specs=row_spec(d),
        out_shape=jax.ShapeDtypeStruct((rows, d), F32),
        scratch_shapes=[pltpu.VMEM((tm + HALO, d), BF16), pltpu.VMEM((tm, d), F32)],
        compiler_params=pltpu.CompilerParams(dimension_semantics=("parallel",), vmem_limit_bytes=VMEM_LIMIT),
        name="ffn",
    )(u2, u2, h1, w_up, conv_w, conv_b, w_down, post_w)


def _layer(hp, pad, attn_pre_norm, w_in, gdn_conv_w, gdn_A_log, gdn_dt_bias, gdn_norm_w, sb_norm_w, w_out,
           attn_post_norm, ffn_pre_norm, w_ffn_up, ffn_conv_w, ffn_conv_b, w_ffn_down, ffn_post_norm):
    b, lp, d = hp.shape
    rows = b * lp
    tm = _pick_tile(lp, (768, 512, 256))
    tg = _pick_tile(lp, (768, 512, 256))
    d_ff = w_ffn_down.shape[0]
    fc = _pick_tile(d_ff, (256, 128))

    o_a = 2 * GDN_QK + D_GDN
    o_z = o_a + 2 * GDN_HEADS
    w_main = jnp.concatenate([w_in[:, :o_a], w_in[:, o_z:]], axis=1).astype(BF16)
    w_ab = jnp.pad(w_in[:, o_a:o_z], ((0, 0), (0, LANES - 2 * GDN_HEADS))).astype(BF16)
    row = lambda v: v.reshape(1, -1).astype(F32)
    lane_row = lambda v: jnp.pad(v.astype(F32), (0, LANES - v.shape[0])).reshape(1, LANES)

    hp2 = hp.reshape(rows, d)
    qkvg, z, qs, ks, vs, ab = _in_proj(hp2, row(attn_pre_norm), w_main, w_ab, tm)
    r3 = lambda t: t.reshape(b, lp, t.shape[-1])
    yg = _gdn(r3(qkvg), r3(ab), r3(z), gdn_conv_w.astype(F32), lane_row(gdn_A_log), lane_row(gdn_dt_bias),
              row(gdn_norm_w), pad, tg)
    ti = lax.broadcasted_iota(jnp.int32, (SB_TILE, SB_TILE), 0)
    tj = lax.broadcasted_iota(jnp.int32, (SB_TILE, SB_TILE), 1)
    tri = (ti >= tj).astype(BF16)
    ys = _sb(r3(qs), r3(ks), r3(vs), row(sb_norm_w), tri, pad)
    h1, u2 = _out_proj(yg.reshape(rows, D_GDN), ys.reshape(rows, D_SB), hp2, w_out[:D_GDN].astype(BF16),
                       w_out[D_GDN:].astype(BF16), row(attn_post_norm), row(ffn_pre_norm), tm)
    out = _ffn(u2, h1, w_ffn_up.astype(BF16), ffn_conv_w.astype(F32), row(ffn_conv_b), w_ffn_down.astype(BF16),
               row(ffn_post_norm), tm, lp // tm, fc, pad)
    return out.reshape(b, lp, d)


def kernel(x, meta_tokens, attn_pre_norm, w_in, gdn_conv_w, gdn_A_log, gdn_dt_bias, gdn_norm_w, sb_norm_w, w_out,
           attn_post_norm, ffn_pre_norm, w_ffn_up, ffn_conv_w, ffn_conv_b, w_ffn_down, ffn_post_norm):
    b, seq, d = x.shape
    depth = w_in.shape[0]
    length = N_META + seq
    pad = (-length) % ROW_ALIGN
    meta = jnp.broadcast_to(meta_tokens[None].astype(x.dtype), (b, N_META, d))
    hp = jnp.concatenate([jnp.zeros((b, pad, d), x.dtype), meta, x], axis=1)
    for l in range(depth):
        hp = _layer(hp, pad, attn_pre_norm[l], w_in[l], gdn_conv_w[l], gdn_A_log[l], gdn_dt_bias[l], gdn_norm_w[l],
                    sb_norm_w[l], w_out[l], attn_post_norm[l], ffn_pre_norm[l], w_ffn_up[l], ffn_conv_w[l],
                    ffn_conv_b[l], w_ffn_down[l], ffn_post_norm[l])
    return hp[:, pad + N_META:]
```

```python
import functools
import itertools
import math

import jax
import jax.numpy as jnp
from jax import lax
from jax.experimental import pallas as pl
from jax.experimental.pallas import tpu as pltpu

F32 = jnp.float32
BF16 = jnp.bfloat16

N_META = 16
GDN_HEADS = 4
GDN_DK = 128
GDN_DV = 128
GDN_CONV = 4
GDN_CHUNK = 64
GDN_QK = GDN_HEADS * GDN_DK
D_GDN = GDN_HEADS * GDN_DV
GDN_CONV_CH = 2 * GDN_QK + D_GDN
SB_HEADS = 8
SB_DH = 64
D_SB = SB_HEADS * SB_DH
FFN_CONV = 3
NORM_EPS = 1e-6
L2_EPS = 1e-6

LANES = 128
SB_TILE = 256
ROW_ALIGN = SB_TILE
HALO = 16
NEG_BIG = -1e30
LOG2E = 1.4426950408889634
VMEM_LIMIT = 56 * 1024 * 1024


def _pick_tile(n, candidates):
    for c in candidates:
        if n % c == 0:
            return c
    raise ValueError(f"no tile for {n}")


def _dot(a, b):
    return jnp.dot(a, b, preferred_element_type=F32)


def _dot_nt(a, b):
    return lax.dot_general(a, b, (((1,), (1,)), ((), ())), preferred_element_type=F32)


def _dot_tn(a, b):
    return lax.dot_general(a, b, (((0,), (0,)), ((), ())), preferred_element_type=F32)


def _split2(x):
    hi = x.astype(BF16)
    lo = (x - hi.astype(F32)).astype(BF16)
    return hi, lo


def _dot_split(a, b):
    ah, al = _split2(a)
    bh, bl = _split2(b)
    return _dot(ah, bh) + (_dot(ah, bl) + _dot(al, bh))


def _silu(x):
    return x * jax.nn.sigmoid(x)


def _softplus(x):
    return jnp.maximum(x, 0.0) + jnp.log(1.0 + jnp.exp(-jnp.abs(x)))


def _const_spec(shape):
    nd = len(shape)
    return pl.BlockSpec(shape, lambda *_: (0,) * nd, pipeline_mode=pl.Buffered(1))


def _gather_rows(src_refs, head_ref, h_scr, batch_start, use_head):
    for s, ref in enumerate(src_refs):
        h_scr[s * ROW_ALIGN:(s + 1) * ROW_ALIGN, :] = ref[...]
    if use_head:
        @pl.when(batch_start)
        def _():
            h_scr[0:ROW_ALIGN, :] = head_ref[...]


def _row_source(hp2, x, meta_tokens, pad, tm, lp):
    nsub = tm // ROW_ALIGN
    d = meta_tokens.shape[-1]
    if hp2 is not None:
        maps = [functools.partial(lambda s, g: (g * nsub + s, 0), s) for s in range(nsub)]
        return hp2, maps, jnp.zeros((ROW_ALIGN, d), F32), False
    b, seq, _ = x.shape
    tiles, src_blocks = lp // tm, seq // ROW_ALIGN
    maps = [functools.partial(
        lambda s, g: ((g // tiles) * src_blocks + jnp.maximum((g % tiles) * nsub + s - 1, 0), 0), s)
        for s in range(nsub)]
    head = jnp.concatenate([jnp.zeros((pad, d), F32), meta_tokens.astype(F32)], axis=0)
    return x.reshape(b * seq, d), maps, head, True


def _in_proj_kernel(*refs, nsub, tiles_per_batch, use_head):
    src_refs, (head_ref, nw_ref, w_ref, wab_ref, qkvg_ref, z_ref, qs_ref, ks_ref, vs_ref, ab_ref,
               h_scr, u_scr) = refs[:nsub], refs[nsub:]
    _gather_rows(src_refs, head_ref, h_scr, pl.program_id(0) % tiles_per_batch == 0, use_head)
    h = h_scr[...]
    ms = jnp.mean(h * h, axis=-1, keepdims=True)
    u_scr[...] = (h * lax.rsqrt(ms + NORM_EPS) * nw_ref[...]).astype(BF16)
    u = u_scr[...]
    c0 = 0
    qkvg_ref[...] = _dot(u, w_ref[:, c0:c0 + GDN_CONV_CH])
    c0 += GDN_CONV_CH
    z_ref[...] = _dot(u, w_ref[:, c0:c0 + D_GDN])
    c0 += D_GDN
    qs_ref[...] = (_dot(u, w_ref[:, c0:c0 + D_SB]) * (SB_DH ** -0.5)).astype(BF16)
    c0 += D_SB
    ks_ref[...] = _dot(u, w_ref[:, c0:c0 + D_SB]).astype(BF16)
    c0 += D_SB
    vs_ref[...] = _dot(u, w_ref[:, c0:c0 + D_SB]).astype(BF16)
    ab_ref[...] = _dot(u, wab_ref[...])


def _in_proj(source, rows, norm_w, w_main, w_ab, tm, tiles_per_batch):
    src, maps, head, use_head = source
    d = src.shape[-1]
    grid = (rows // tm,)
    row_spec = lambda n: pl.BlockSpec((tm, n), lambda i: (i, 0))
    kern = functools.partial(_in_proj_kernel, nsub=len(maps), tiles_per_batch=tiles_per_batch, use_head=use_head)
    return pl.pallas_call(
        kern,
        grid=grid,
        in_specs=[pl.BlockSpec((ROW_ALIGN, d), m) for m in maps]
                 + [_const_spec(head.shape), _const_spec((1, d)), _const_spec(w_main.shape), _const_spec(w_ab.shape)],
        out_specs=[row_spec(GDN_CONV_CH), row_spec(D_GDN), row_spec(D_SB), row_spec(D_SB), row_spec(D_SB),
                   row_spec(LANES)],
        out_shape=[jax.ShapeDtypeStruct((rows, GDN_CONV_CH), F32), jax.ShapeDtypeStruct((rows, D_GDN), F32),
                   jax.ShapeDtypeStruct((rows, D_SB), BF16), jax.ShapeDtypeStruct((rows, D_SB), BF16),
                   jax.ShapeDtypeStruct((rows, D_SB), BF16), jax.ShapeDtypeStruct((rows, LANES), F32)],
        scratch_shapes=[pltpu.VMEM((tm, d), F32), pltpu.VMEM((tm, d), BF16)],
        compiler_params=pltpu.CompilerParams(dimension_semantics=("parallel",), vmem_limit_bytes=VMEM_LIMIT),
        name="in_proj",
    )(*([src] * len(maps)), head, norm_w, w_main, w_ab)


def _gdn_kernel(qkv_ref, ab_ref, z_ref, cw_ref, alog_ref, dtb_ref, nw_ref, y_ref,
                xbuf, qkvc, g_scr, beta_scr, o_scr, s_scr, pu_scr, pw_scr, pqk_scr, pqd_scr, pkd_scr, pgl_scr,
                *, pad, tg):
    i = pl.program_id(1)
    pair = 2 * GDN_CHUNK
    c = GDN_CHUNK

    @pl.when(i == 0)
    def _():
        s_scr[...] = jnp.zeros_like(s_scr)
        xbuf[0:HALO, :] = jnp.zeros((HALO, GDN_CONV_CH), F32)

    xbuf[HALO:HALO + tg, :] = qkv_ref[0]
    for grp in range(GDN_CONV_CH // LANES):
        cols = slice(grp * LANES, (grp + 1) * LANES)
        xe = xbuf[:, cols]
        acc = xe * cw_ref[GDN_CONV - 1:GDN_CONV, cols]
        for sh in range(1, GDN_CONV):
            acc = acc + pltpu.roll(xe, sh, axis=0) * cw_ref[GDN_CONV - 1 - sh:GDN_CONV - sh, cols]
        y = _silu(acc[HALO:, :])
        if grp < 2 * GDN_HEADS:
            y = y * lax.rsqrt(jnp.sum(y * y, axis=-1, keepdims=True) + L2_EPS)
            if grp < GDN_HEADS:
                y = y * (GDN_DK ** -0.5)
        qkvc[:, cols] = y
    xbuf[0:HALO, :] = xbuf[tg:tg + HALO, :]

    ab = ab_ref[0]
    rows = i * tg + lax.broadcasted_iota(jnp.int32, (tg, LANES), 0)
    valid = rows >= pad
    g = -jnp.exp(alog_ref[...]) * _softplus(ab + dtb_ref[...])
    g_scr[...] = jnp.where(valid, g, 0.0)
    beta_scr[...] = jnp.where(valid, jax.nn.sigmoid(pltpu.roll(ab, LANES - GDN_HEADS, axis=1)), 0.0)

    ri = lax.broadcasted_iota(jnp.int32, (pair, pair), 0)
    ci = lax.broadcasted_iota(jnp.int32, (pair, pair), 1)
    same_chunk = jnp.right_shift(ri, int(math.log2(c))) == jnp.right_shift(ci, int(math.log2(c)))
    incl = (ci <= ri) & same_chunk
    strict = (ci < ri) & same_chunk
    cum_mat = jnp.where(incl, 1.0, 0.0).astype(BF16)
    eye = jnp.where(ri == ci, 1.0, 0.0)
    upper_rows = lax.broadcasted_iota(jnp.int32, (pair, 1), 0) < c
    heads = range(GDN_HEADS)

    n_pairs = tg // pair

    def prepare(p, slot):
        r0 = pl.multiple_of(p * pair, pair)
        rows_p = pl.ds(r0, pair)
        gp = g_scr[rows_p, :]
        g_hi = gp.astype(BF16)
        g_r1 = gp - g_hi.astype(F32)
        g_mid = g_r1.astype(BF16)
        g_lo = (g_r1 - g_mid.astype(F32)).astype(BF16)
        gc = _dot(cum_mat, g_hi) + (_dot(cum_mat, g_mid) + _dot(cum_mat, g_lo))
        gct = gc.T
        bp = beta_scr[rows_p, :]
        g_col = [gc[:, hd:hd + 1] for hd in heads]
        g_last = [[gc[s * c + c - 1:s * c + c, hd:hd + 1] for s in range(2)] for hd in heads]
        b_col = [bp[:, hd:hd + 1] for hd in heads]
        qc = [qkvc[rows_p, hd * GDN_DK:(hd + 1) * GDN_DK] for hd in heads]
        kc = [qkvc[rows_p, GDN_QK + hd * GDN_DK:GDN_QK + (hd + 1) * GDN_DK] for hd in heads]
        vc = [qkvc[rows_p, 2 * GDN_QK + hd * GDN_DV:2 * GDN_QK + (hd + 1) * GDN_DV] for hd in heads]
        decay = [jnp.exp(jnp.where(incl, g_col[hd] - gct[hd:hd + 1, :], NEG_BIG)) for hd in heads]
        eg = [jnp.exp(g_col[hd]) for hd in heads]
        kb = [kc[hd] * b_col[hd] for hd in heads]
        kc16 = [kc[hd].astype(BF16) for hd in heads]
        a_mat = [jnp.where(strict, _dot_nt(kb[hd].astype(BF16), kc16[hd]) * decay[hd], 0.0) for hd in heads]
        pw = [-a for a in a_mat]
        t_mat = [eye + x for x in pw]
        yield
        for _ in range(int(math.log2(c)) - 1):
            pw16 = [x.astype(BF16) for x in pw]
            pw = [_dot(x, x) for x in pw16]
            t_mat = [tm + _dot(tm.astype(BF16), x.astype(BF16)) for tm, x in zip(t_mat, pw)]
            yield
        t16 = [tm.astype(BF16) for tm in t_mat]
        for hd in heads:
            pu_scr[slot, hd] = _dot(t16[hd], (vc[hd] * b_col[hd]).astype(BF16))
            pw_scr[slot, hd] = _dot(t16[hd], (kb[hd] * eg[hd]).astype(BF16)).astype(BF16)
            pqk_scr[slot, hd] = (_dot_nt(qc[hd].astype(BF16), kc16[hd]) * decay[hd]).astype(BF16)
            pqd_scr[slot, hd] = (qc[hd] * eg[hd]).astype(BF16)
            pkd_scr[slot, hd] = (kc[hd] * jnp.exp(jnp.where(upper_rows, g_last[hd][0], g_last[hd][1])
                                                  - g_col[hd])).astype(BF16)
            for s in range(2):
                pgl_scr[slot, hd, s] = jnp.broadcast_to(jnp.exp(g_last[hd][s]), (8, LANES))
        yield

    def advance(p, slot):
        r0 = pl.multiple_of(p * pair, pair)
        zeros_c = jnp.zeros((c, GDN_DV), BF16)
        for s in range(2):
            rs = slice(s * c, (s + 1) * c)
            s_old = [s_scr[hd] for hd in heads]
            s16 = [x.astype(BF16) for x in s_old]
            v16 = [(pu_scr[slot, hd, rs, :] - _dot(pw_scr[slot, hd, rs, :], s16[hd])).astype(BF16) for hd in heads]
            yield
            vpad = [jnp.concatenate([v16[hd], zeros_c] if s == 0 else [zeros_c, v16[hd]], axis=0) for hd in heads]
            o = [_dot(pqd_scr[slot, hd, rs, :], s16[hd]) + _dot(pqk_scr[slot, hd, rs, :], vpad[hd]) for hd in heads]
            for hd in heads:
                s_scr[hd] = s_old[hd] * pgl_scr[slot, hd, s, 0:1, :] + _dot_tn(pkd_scr[slot, hd, rs, :], v16[hd])
                o_scr[pl.ds(r0 + s * c, c), hd * GDN_DV:(hd + 1) * GDN_DV] = o[hd]
            yield

    for _ in prepare(0, 0):
        pass

    def pair_body(p, carry):
        nxt = jnp.minimum(p + 1, n_pairs - 1)
        for _ in itertools.zip_longest(prepare(nxt, (p + 1) % 2), advance(p, p % 2)):
            pass
        return carry

    lax.fori_loop(0, n_pairs, pair_body, 0)

    for hd in range(GDN_HEADS):
        cols = slice(hd * GDN_DV, (hd + 1) * GDN_DV)
        o = o_scr[:, cols]
        ms = jnp.mean(o * o, axis=-1, keepdims=True)
        y = o * lax.rsqrt(ms + NORM_EPS) * nw_ref[...]
        y_ref[0, :, cols] = (y * _silu(z_ref[0, :, cols])).astype(BF16)


def _gdn(qkv, ab, z, conv_w, alog_row, dtb_row, norm_row, pad, tg):
    b, lp, _ = qkv.shape
    grid = (b, lp // tg)
    tile = lambda n: pl.BlockSpec((1, tg, n), lambda bi, i: (bi, i, 0))
    kern = functools.partial(_gdn_kernel, pad=pad, tg=tg)
    return pl.pallas_call(
        kern,
        grid=grid,
        in_specs=[tile(GDN_CONV_CH), tile(LANES), tile(D_GDN), _const_spec(conv_w.shape),
                  _const_spec((1, LANES)), _const_spec((1, LANES)), _const_spec((1, GDN_DV))],
        out_specs=tile(D_GDN),
        out_shape=jax.ShapeDtypeStruct((b, lp, D_GDN), BF16),
        scratch_shapes=[pltpu.VMEM((tg + HALO, GDN_CONV_CH), F32),
                        pltpu.VMEM((tg, GDN_CONV_CH), F32),
                        pltpu.VMEM((tg, LANES), F32),
                        pltpu.VMEM((tg, LANES), F32),
                        pltpu.VMEM((tg, D_GDN), F32),
                        pltpu.VMEM((GDN_HEADS, GDN_DK, GDN_DV), F32),
                        pltpu.VMEM((2, GDN_HEADS, 2 * GDN_CHUNK, GDN_DV), F32),
                        pltpu.VMEM((2, GDN_HEADS, 2 * GDN_CHUNK, GDN_DK), BF16),
                        pltpu.VMEM((2, GDN_HEADS, 2 * GDN_CHUNK, 2 * GDN_CHUNK), BF16),
                        pltpu.VMEM((2, GDN_HEADS, 2 * GDN_CHUNK, GDN_DK), BF16),
                        pltpu.VMEM((2, GDN_HEADS, 2 * GDN_CHUNK, GDN_DK), BF16),
                        pltpu.VMEM((2, GDN_HEADS, 2, 8, LANES), F32)],
        compiler_params=pltpu.CompilerParams(dimension_semantics=("arbitrary", "arbitrary"),
                                             vmem_limit_bytes=VMEM_LIMIT),
        name="gdn",
    )(qkv, ab, z, conv_w, alog_row, dtb_row, norm_row)


def _sb_kernel(q_ref, k_ref, v_ref, nw_ref, tri_ref, y_ref, qm_scr, acc_scr, run_scr, *, pad):
    i = pl.program_id(1)
    t = SB_TILE
    nh = SB_HEADS
    lane = lax.broadcasted_iota(jnp.int32, (t, LANES), 1)
    first = lane < SB_DH
    pair_cols = lambda hd: slice((hd // 2) * LANES, (hd // 2 + 1) * LANES)
    for hd in range(nh):
        qp = q_ref[0, :, pair_cols(hd)]
        zero = jnp.zeros_like(qp)
        qm_scr[hd] = jnp.where(first, qp, zero) if hd % 2 == 0 else jnp.where(first, zero, qp)
    tri = tri_ref[...]
    acc_scr[...] = jnp.zeros_like(acc_scr)
    run_scr[...] = jnp.zeros_like(run_scr)
    qi = lax.broadcasted_iota(jnp.int32, (t, t), 0)
    ki = lax.broadcasted_iota(jnp.int32, (t, t), 1)

    def blocks(js, mask):
        items = [(pl.ds(pl.multiple_of(j * t, t), t), hd) for j in js for hd in range(nh)]
        n = len(items)
        z, sp16, cs, att = ({} for _ in range(4))
        for step in range(n + 4):
            it = step
            if it < n:
                rows, hd = items[it]
                z[it] = _dot_nt(qm_scr[hd], k_ref[0, rows, pair_cols(hd)])
            it = step - 1
            if 0 <= it < n:
                zb = z[it].astype(BF16)
                sp = jnp.maximum(zb, 0.0) + jnp.log(1.0 + jnp.exp2(jnp.abs(zb) * (-LOG2E)))
                if mask is not None:
                    sp = jnp.where(mask, sp, jnp.zeros_like(sp))
                sp16[it] = sp
            it = step - 2
            if 0 <= it < n:
                cs[it] = _dot(sp16[it], tri)
            it = step - 3
            if 0 <= it < n:
                hd = items[it][1]
                run = run_scr[hd]
                e = (z[it] - cs[it]) - jnp.concatenate([run, run], axis=1)
                if mask is not None:
                    e = jnp.where(mask, e, NEG_BIG)
                att[it] = jnp.exp(e).astype(BF16)
                run_scr[hd] = run + cs[it][:, 0:1]
            it = step - 4
            if 0 <= it < n:
                rows, hd = items[it]
                acc_scr[hd] += _dot(att[it], v_ref[0, rows, pair_cols(hd)])

    blocks([i], (ki < qi) & (ki + i * t >= pad))

    n_inner = jnp.maximum(i - 1, 0)

    def inner_quad(n, carry):
        top = i - 1 - 4 * n
        blocks([top, top - 1, top - 2, top - 3], None)
        return carry

    lax.fori_loop(0, n_inner // 4, inner_quad, 0)
    rem = n_inner % 4

    @pl.when(rem >= 2)
    def _():
        blocks([rem, rem - 1], None)

    @pl.when(rem % 2 == 1)
    def _():
        blocks([1], None)

    @pl.when(i >= 1)
    def _():
        blocks([0], ki >= pad)

    for pr in range(nh // 2):
        o = jnp.where(first, acc_scr[2 * pr], acc_scr[2 * pr + 1])
        sq = o * o
        ss_first = jnp.sum(jnp.where(first, sq, 0.0), axis=-1, keepdims=True)
        ss_all = jnp.sum(sq, axis=-1, keepdims=True)
        ms = jnp.where(first, ss_first, ss_all - ss_first) * (1.0 / SB_DH)
        cols = slice(pr * LANES, (pr + 1) * LANES)
        y_ref[0, :, cols] = (o * lax.rsqrt(ms + NORM_EPS) * nw_ref[:, cols]).astype(BF16)


def _sb(q, k, v, norm_row, tri, pad):
    b, lp, _ = q.shape
    t = SB_TILE
    kern = functools.partial(_sb_kernel, pad=pad)
    full = pl.BlockSpec((1, lp, D_SB), lambda bi, i: (bi, 0, 0), pipeline_mode=pl.Buffered(1))
    tile = pl.BlockSpec((1, t, D_SB), lambda bi, i: (bi, i, 0))
    return pl.pallas_call(
        kern,
        grid=(b, lp // t),
        in_specs=[tile, full, full, _const_spec((1, D_SB)), _const_spec((t, t))],
        out_specs=tile,
        out_shape=jax.ShapeDtypeStruct((b, lp, D_SB), BF16),
        scratch_shapes=[pltpu.VMEM((SB_HEADS, t, LANES), BF16),
                        pltpu.VMEM((SB_HEADS, t, LANES), F32),
                        pltpu.VMEM((SB_HEADS, t, LANES), F32)],
        compiler_params=pltpu.CompilerParams(dimension_semantics=("parallel", "arbitrary"),
                                             vmem_limit_bytes=VMEM_LIMIT),
        name="sb",
    )(q, k, v, norm_row, tri)


def _out_proj_kernel(*refs, nsub, tiles_per_batch, use_head):
    src_refs, (head_ref, yg_ref, ys_ref, wg_ref, ws_ref, post_ref, pre_ref, h1_ref, u2_ref,
               h_scr) = refs[:nsub], refs[nsub:]
    _gather_rows(src_refs, head_ref, h_scr, pl.program_id(0) % tiles_per_batch == 0, use_head)
    mix = _dot(yg_ref[...], wg_ref[...]) + _dot(ys_ref[...], ws_ref[...])
    ms = jnp.mean(mix * mix, axis=-1, keepdims=True)
    h1 = h_scr[...] + mix * lax.rsqrt(ms + NORM_EPS) * post_ref[...]
    h1_ref[...] = h1
    ms1 = jnp.mean(h1 * h1, axis=-1, keepdims=True)
    u2_ref[...] = (h1 * lax.rsqrt(ms1 + NORM_EPS) * pre_ref[...]).astype(BF16)


def _out_proj(yg, ys, source, wg, ws, post_w, pre_w, tm, tiles_per_batch):
    src, maps, head, use_head = source
    rows, d = yg.shape[0], src.shape[-1]
    row_spec = lambda n: pl.BlockSpec((tm, n), lambda i: (i, 0))
    kern = functools.partial(_out_proj_kernel, nsub=len(maps), tiles_per_batch=tiles_per_batch, use_head=use_head)
    return pl.pallas_call(
        kern,
        grid=(rows // tm,),
        in_specs=[pl.BlockSpec((ROW_ALIGN, d), m) for m in maps]
                 + [_const_spec(head.shape), row_spec(D_GDN), row_spec(D_SB), _const_spec(wg.shape),
                    _const_spec(ws.shape), _const_spec((1, d)), _const_spec((1, d))],
        out_specs=[row_spec(d), row_spec(d)],
        out_shape=[jax.ShapeDtypeStruct((rows, d), F32), jax.ShapeDtypeStruct((rows, d), BF16)],
        scratch_shapes=[pltpu.VMEM((tm, d), F32)],
        compiler_params=pltpu.CompilerParams(dimension_semantics=("parallel",), vmem_limit_bytes=VMEM_LIMIT),
        name="out_proj",
    )(*([src] * len(maps)), head, yg, ys, wg, ws, post_w, pre_w)


def _ffn_kernel(u_ref, halo_ref, h1_ref, wup_ref, cw_ref, cb_ref, wd_ref, post_ref, out_ref, uext, acc,
                *, tm, tiles_per_batch, d_ff, fc, pad, final):
    i = pl.program_id(0)
    if final:
        uext[0:HALO, :] = halo_ref[...]
    else:
        keep = jnp.where(i % tiles_per_batch == 0, 0.0, 1.0).astype(BF16)
        uext[0:HALO, :] = halo_ref[...] * keep
    uext[HALO:, :] = u_ref[...]
    ue = uext[...]

    def conv(hx, cols):
        y = hx * cw_ref[FFN_CONV - 1:FFN_CONV, cols]
        for sh in range(1, FFN_CONV):
            y = y + pltpu.roll(hx, sh, axis=0) * cw_ref[FFN_CONV - 1 - sh:FFN_CONV - sh, cols]
        return y[HALO:, :] + cb_ref[:, cols]

    n_chunks = d_ff // fc
    gcols = [slice(c * fc, (c + 1) * fc) for c in range(n_chunks)]
    ucols = [slice(d_ff + c * fc, d_ff + (c + 1) * fc) for c in range(n_chunks)]
    hg, hu, act = {}, {}, {}
    for step in range(n_chunks + 2):
        c = step
        if c < n_chunks:
            hg[c] = _dot(ue, wup_ref[:, gcols[c]])
            hu[c] = _dot(ue, wup_ref[:, ucols[c]])
        c = step - 1
        if 0 <= c < n_chunks:
            gate = conv(hg.pop(c), gcols[c])
            up = conv(hu.pop(c), ucols[c])
            act[c] = (jax.nn.gelu(gate, approximate=True) * up).astype(BF16)
        c = step - 2
        if 0 <= c < n_chunks:
            part = _dot(act.pop(c), wd_ref[gcols[c], :])
            if c == 0:
                acc[...] = part
            else:
                acc[...] += part

    f = acc[...]
    ms = jnp.mean(f * f, axis=-1, keepdims=True)
    out = h1_ref[...] + f * lax.rsqrt(ms + NORM_EPS) * post_ref[...]
    if not final:
        t = (i % tiles_per_batch) * tm + lax.broadcasted_iota(jnp.int32, out.shape, 0)
        out = jnp.where(t >= pad, out, 0.0)
    out_ref[...] = out


def _ffn(u2, h1, w_up, conv_w, conv_b, w_down, post_w, tm, lp, seq, fc, pad, final):
    rows, d = h1.shape
    b = rows // lp
    d_ff = w_down.shape[0]
    if final:
        tiles, out_rows = seq // tm, b * seq
        first = lambda g: (g // tiles) * (lp // tm) + (lp - seq) // tm + g % tiles
    else:
        tiles, out_rows = lp // tm, rows
        first = lambda g: g
    kern = functools.partial(_ffn_kernel, tm=tm, tiles_per_batch=tiles, d_ff=d_ff, fc=fc, pad=pad, final=final)
    in_spec = pl.BlockSpec((tm, d), lambda g: (first(g), 0))
    halo_spec = pl.BlockSpec((HALO, d), lambda g: (jnp.maximum(first(g) * (tm // HALO) - 1, 0), 0))
    return pl.pallas_call(
        kern,
        grid=(out_rows // tm,),
        in_specs=[in_spec, halo_spec, in_spec, _const_spec(w_up.shape), _const_spec(conv_w.shape),
                  _const_spec(conv_b.shape), _const_spec(w_down.shape), _const_spec((1, d))],
        out_specs=pl.BlockSpec((tm, d), lambda g: (g, 0)),
        out_shape=jax.ShapeDtypeStruct((out_rows, d), F32),
        scratch_shapes=[pltpu.VMEM((tm + HALO, d), BF16), pltpu.VMEM((tm, d), F32)],
        compiler_params=pltpu.CompilerParams(dimension_semantics=("parallel",), vmem_limit_bytes=VMEM_LIMIT),
        name="ffn",
    )(u2, u2, h1, w_up, conv_w, conv_b, w_down, post_w)


def _layer(hp, x, meta_tokens, pad, lp, last, attn_pre_norm, w_in, gdn_conv_w, gdn_A_log, gdn_dt_bias, gdn_norm_w,
           sb_norm_w, w_out, attn_post_norm, ffn_pre_norm, w_ffn_up, ffn_conv_w, ffn_conv_b, w_ffn_down,
           ffn_post_norm):
    b, _, d = x.shape
    rows = b * lp
    tm = _pick_tile(lp, (768, 512, 256))
    tg = _pick_tile(lp, (768, 512, 256))
    d_ff = w_ffn_down.shape[0]
    fc = _pick_tile(d_ff, (256, 128))

    o_a = 2 * GDN_QK + D_GDN
    o_z = o_a + 2 * GDN_HEADS
    w_main = jnp.concatenate([w_in[:, :o_a], w_in[:, o_z:]], axis=1).astype(BF16)
    w_ab = jnp.pad(w_in[:, o_a:o_z], ((0, 0), (0, LANES - 2 * GDN_HEADS))).astype(BF16)
    row = lambda v: v.reshape(1, -1).astype(F32)
    lane_row = lambda v: jnp.pad(v.astype(F32), (0, LANES - v.shape[0])).reshape(1, LANES)

    source = _row_source(None if hp is None else hp.reshape(rows, d), x, meta_tokens, pad, tm, lp)
    qkvg, z, qs, ks, vs, ab = _in_proj(source, rows, row(attn_pre_norm), w_main, w_ab, tm, lp // tm)
    r3 = lambda t: t.reshape(b, lp, t.shape[-1])
    yg = _gdn(r3(qkvg), r3(ab), r3(z), gdn_conv_w.astype(F32), lane_row(gdn_A_log), lane_row(gdn_dt_bias),
              row(gdn_norm_w), pad, tg)
    ti = lax.broadcasted_iota(jnp.int32, (SB_TILE, SB_TILE), 0)
    tj = lax.broadcasted_iota(jnp.int32, (SB_TILE, SB_TILE), 1)
    tri = (ti >= tj).astype(BF16)
    ys = _sb(r3(qs), r3(ks), r3(vs), row(sb_norm_w), tri, pad)
    h1, u2 = _out_proj(yg.reshape(rows, D_GDN), ys.reshape(rows, D_SB), source, w_out[:D_GDN].astype(BF16),
                       w_out[D_GDN:].astype(BF16), row(attn_post_norm), row(ffn_pre_norm), tm, lp // tm)
    seq = x.shape[1]
    tf = ROW_ALIGN
    final = last and (pad + N_META) % tf == 0 and seq % tf == 0
    out = _ffn(u2, h1, w_ffn_up.astype(BF16), ffn_conv_w.astype(F32), row(ffn_conv_b), w_ffn_down.astype(BF16),
               row(ffn_post_norm), tf, lp, seq, fc, pad, final)
    return out.reshape(b, seq if final else lp, d)


def kernel(x, meta_tokens, attn_pre_norm, w_in, gdn_conv_w, gdn_A_log, gdn_dt_bias, gdn_norm_w, sb_norm_w, w_out,
           attn_post_norm, ffn_pre_norm, w_ffn_up, ffn_conv_w, ffn_conv_b, w_ffn_down, ffn_post_norm):
    b, seq, d = x.shape
    depth = w_in.shape[0]
    length = N_META + seq
    pad = (-length) % ROW_ALIGN
    lp = length + pad
    hp = None
    if pad + N_META != ROW_ALIGN:
        meta = jnp.broadcast_to(meta_tokens[None].astype(x.dtype), (b, N_META, d))
        hp = jnp.concatenate([jnp.zeros((b, pad, d), x.dtype), meta, x], axis=1)
    for l in range(depth):
        hp = _layer(hp, x, meta_tokens, pad, lp, l == depth - 1, attn_pre_norm[l], w_in[l], gdn_conv_w[l],
                    gdn_A_log[l], gdn_dt_bias[l], gdn_norm_w[l], sb_norm_w[l], w_out[l], attn_post_norm[l],
                    ffn_pre_norm[l], w_ffn_up[l], ffn_conv_w[l], ffn_conv_b[l], w_ffn_down[l], ffn_post_norm[l])
    return hp if hp.shape[1] == seq else hp[:, pad + N_META:]
```

```python
import functools
import itertools
import math

import jax
import jax.numpy as jnp
from jax import lax
from jax.experimental import pallas as pl
from jax.experimental.pallas import tpu as pltpu

F32 = jnp.float32
BF16 = jnp.bfloat16

N_META = 16
GDN_HEADS = 4
GDN_DK = 128
GDN_DV = 128
GDN_CONV = 4
GDN_CHUNK = 64
GDN_QK = GDN_HEADS * GDN_DK
D_GDN = GDN_HEADS * GDN_DV
GDN_CONV_CH = 2 * GDN_QK + D_GDN
SB_HEADS = 8
SB_DH = 64
D_SB = SB_HEADS * SB_DH
FFN_CONV = 3
NORM_EPS = 1e-6
L2_EPS = 1e-6

LANES = 128
SB_TILE = 256
ROW_ALIGN = SB_TILE
HALO = 16
NEG_BIG = -1e30
LOG2E = 1.4426950408889634
VMEM_LIMIT = 56 * 1024 * 1024


def _pick_tile(n, candidates):
    for c in candidates:
        if n % c == 0:
            return c
    raise ValueError(f"no tile for {n}")


def _dot(a, b):
    return jnp.dot(a, b, preferred_element_type=F32)


def _dot_nt(a, b):
    return lax.dot_general(a, b, (((1,), (1,)), ((), ())), preferred_element_type=F32)


def _dot_tn(a, b):
    return lax.dot_general(a, b, (((0,), (0,)), ((), ())), preferred_element_type=F32)


def _split2(x):
    hi = x.astype(BF16)
    lo = (x - hi.astype(F32)).astype(BF16)
    return hi, lo


def _dot_split(a, b):
    ah, al = _split2(a)
    bh, bl = _split2(b)
    return _dot(ah, bh) + (_dot(ah, bl) + _dot(al, bh))


def _silu(x):
    return x * jax.nn.sigmoid(x)


def _softplus(x):
    return jnp.maximum(x, 0.0) + jnp.log(1.0 + jnp.exp(-jnp.abs(x)))


def _const_spec(shape):
    nd = len(shape)
    return pl.BlockSpec(shape, lambda *_: (0,) * nd, pipeline_mode=pl.Buffered(1))


def _gather_rows(src_refs, head_ref, h_scr, batch_start, use_head):
    for s, ref in enumerate(src_refs):
        h_scr[s * ROW_ALIGN:(s + 1) * ROW_ALIGN, :] = ref[...]
    if use_head:
        @pl.when(batch_start)
        def _():
            h_scr[0:ROW_ALIGN, :] = head_ref[...]


def _row_source(hp2, x, meta_tokens, pad, tm, lp):
    nsub = tm // ROW_ALIGN
    d = meta_tokens.shape[-1]
    if hp2 is not None:
        maps = [functools.partial(lambda s, g: (g * nsub + s, 0), s) for s in range(nsub)]
        return hp2, maps, jnp.zeros((ROW_ALIGN, d), F32), False
    b, seq, _ = x.shape
    tiles, src_blocks = lp // tm, seq // ROW_ALIGN
    maps = [functools.partial(
        lambda s, g: ((g // tiles) * src_blocks + jnp.maximum((g % tiles) * nsub + s - 1, 0), 0), s)
        for s in range(nsub)]
    head = jnp.concatenate([jnp.zeros((pad, d), F32), meta_tokens.astype(F32)], axis=0)
    return x.reshape(b * seq, d), maps, head, True


def _in_proj_kernel(*refs, nsub, tiles_per_batch, use_head):
    src_refs, (head_ref, nw_ref, w_ref, wab_ref, qkvg_ref, z_ref, qs_ref, ks_ref, vs_ref, ab_ref,
               h_scr, u_scr) = refs[:nsub], refs[nsub:]
    _gather_rows(src_refs, head_ref, h_scr, pl.program_id(0) % tiles_per_batch == 0, use_head)
    h = h_scr[...]
    ms = jnp.mean(h * h, axis=-1, keepdims=True)
    u_scr[...] = (h * lax.rsqrt(ms + NORM_EPS) * nw_ref[...]).astype(BF16)
    u = u_scr[...]
    c0 = 0
    qkvg_ref[...] = _dot(u, w_ref[:, c0:c0 + GDN_CONV_CH])
    c0 += GDN_CONV_CH
    z_ref[...] = _dot(u, w_ref[:, c0:c0 + D_GDN])
    c0 += D_GDN
    qs_ref[...] = (_dot(u, w_ref[:, c0:c0 + D_SB]) * (SB_DH ** -0.5)).astype(BF16)
    c0 += D_SB
    ks_ref[...] = _dot(u, w_ref[:, c0:c0 + D_SB]).astype(BF16)
    c0 += D_SB
    vs_ref[...] = _dot(u, w_ref[:, c0:c0 + D_SB]).astype(BF16)
    ab_ref[...] = _dot(u, wab_ref[...])


def _in_proj(source, rows, norm_w, w_main, w_ab, tm, tiles_per_batch):
    src, maps, head, use_head = source
    d = src.shape[-1]
    grid = (rows // tm,)
    row_spec = lambda n: pl.BlockSpec((tm, n), lambda i: (i, 0))
    kern = functools.partial(_in_proj_kernel, nsub=len(maps), tiles_per_batch=tiles_per_batch, use_head=use_head)
    return pl.pallas_call(
        kern,
        grid=grid,
        in_specs=[pl.BlockSpec((ROW_ALIGN, d), m) for m in maps]
                 + [_const_spec(head.shape), _const_spec((1, d)), _const_spec(w_main.shape), _const_spec(w_ab.shape)],
        out_specs=[row_spec(GDN_CONV_CH), row_spec(D_GDN), row_spec(D_SB), row_spec(D_SB), row_spec(D_SB),
                   row_spec(LANES)],
        out_shape=[jax.ShapeDtypeStruct((rows, GDN_CONV_CH), F32), jax.ShapeDtypeStruct((rows, D_GDN), F32),
                   jax.ShapeDtypeStruct((rows, D_SB), BF16), jax.ShapeDtypeStruct((rows, D_SB), BF16),
                   jax.ShapeDtypeStruct((rows, D_SB), BF16), jax.ShapeDtypeStruct((rows, LANES), F32)],
        scratch_shapes=[pltpu.VMEM((tm, d), F32), pltpu.VMEM((tm, d), BF16)],
        compiler_params=pltpu.CompilerParams(dimension_semantics=("parallel",), vmem_limit_bytes=VMEM_LIMIT),
        name="in_proj",
    )(*([src] * len(maps)), head, norm_w, w_main, w_ab)


def _gdn_kernel(qkv_ref, ab_ref, z_ref, cw_ref, alog_ref, dtb_ref, nw_ref, y_ref,
                xbuf, qkvc, g_scr, beta_scr, o_scr, s_scr, pu_scr, pw_scr, pqk_scr, pqd_scr, pkd_scr, pgl_scr,
                *, pad, tg, nb):
    i = pl.program_id(1)
    pair = 2 * GDN_CHUNK
    c = GDN_CHUNK

    @pl.when(i == 0)
    def _():
        s_scr[...] = jnp.zeros_like(s_scr)
        xbuf[:, 0:HALO, :] = jnp.zeros((nb, HALO, GDN_CONV_CH), F32)

    rows = i * tg + lax.broadcasted_iota(jnp.int32, (tg, LANES), 0)
    valid = rows >= pad
    for bb in range(nb):
        xbuf[bb, HALO:HALO + tg, :] = qkv_ref[bb]
        for grp in range(GDN_CONV_CH // LANES):
            cols = slice(grp * LANES, (grp + 1) * LANES)
            xe = xbuf[bb, :, cols]
            acc = xe * cw_ref[GDN_CONV - 1:GDN_CONV, cols]
            for sh in range(1, GDN_CONV):
                acc = acc + pltpu.roll(xe, sh, axis=0) * cw_ref[GDN_CONV - 1 - sh:GDN_CONV - sh, cols]
            y = _silu(acc[HALO:, :])
            if grp < 2 * GDN_HEADS:
                y = y * lax.rsqrt(jnp.sum(y * y, axis=-1, keepdims=True) + L2_EPS)
                if grp < GDN_HEADS:
                    y = y * (GDN_DK ** -0.5)
            qkvc[bb, :, cols] = y
        xbuf[bb, 0:HALO, :] = xbuf[bb, tg:tg + HALO, :]

        ab = ab_ref[bb]
        g = -jnp.exp(alog_ref[...]) * _softplus(ab + dtb_ref[...])
        g_scr[bb] = jnp.where(valid, g, 0.0)
        beta_scr[bb] = jnp.where(valid, jax.nn.sigmoid(pltpu.roll(ab, LANES - GDN_HEADS, axis=1)), 0.0)

    ri = lax.broadcasted_iota(jnp.int32, (pair, pair), 0)
    ci = lax.broadcasted_iota(jnp.int32, (pair, pair), 1)
    same_chunk = jnp.right_shift(ri, int(math.log2(c))) == jnp.right_shift(ci, int(math.log2(c)))
    incl = (ci <= ri) & same_chunk
    strict = (ci < ri) & same_chunk
    cum_mat = jnp.where(incl, 1.0, 0.0).astype(BF16)
    eye = jnp.where(ri == ci, 1.0, 0.0)
    upper_rows = lax.broadcasted_iota(jnp.int32, (pair, 1), 0) < c
    chains = [(bb, hd) for bb in range(nb) for hd in range(GDN_HEADS)]
    heads = range(len(chains))

    n_pairs = tg // pair

    def prepare(p, slot):
        r0 = pl.multiple_of(p * pair, pair)
        rows_p = pl.ds(r0, pair)
        gcs, gcts = [], []
        for bb in range(nb):
            gp = g_scr[bb, rows_p, :]
            g_hi = gp.astype(BF16)
            g_r1 = gp - g_hi.astype(F32)
            g_mid = g_r1.astype(BF16)
            g_lo = (g_r1 - g_mid.astype(F32)).astype(BF16)
            gc = _dot(cum_mat, g_hi) + (_dot(cum_mat, g_mid) + _dot(cum_mat, g_lo))
            gcs.append(gc)
            gcts.append(gc.T)
        g_col = [gcs[bb][:, hd:hd + 1] for bb, hd in chains]
        g_last = [[gcs[bb][s * c + c - 1:s * c + c, hd:hd + 1] for s in range(2)] for bb, hd in chains]
        b_col = [beta_scr[bb, rows_p, :][:, hd:hd + 1] for bb, hd in chains]
        qc = [qkvc[bb, rows_p, hd * GDN_DK:(hd + 1) * GDN_DK] for bb, hd in chains]
        kc = [qkvc[bb, rows_p, GDN_QK + hd * GDN_DK:GDN_QK + (hd + 1) * GDN_DK] for bb, hd in chains]
        vc = [qkvc[bb, rows_p, 2 * GDN_QK + hd * GDN_DV:2 * GDN_QK + (hd + 1) * GDN_DV] for bb, hd in chains]
        decay = [jnp.exp(jnp.where(incl, g_col[ch] - gcts[bb][hd:hd + 1, :], NEG_BIG))
                 for ch, (bb, hd) in enumerate(chains)]
        eg = [jnp.exp(g_col[hd]) for hd in heads]
        kb = [kc[hd] * b_col[hd] for hd in heads]
        kc16 = [kc[hd].astype(BF16) for hd in heads]
        a_mat = [jnp.where(strict, _dot_nt(kb[hd].astype(BF16), kc16[hd]) * decay[hd], 0.0) for hd in heads]
        pw = [-a for a in a_mat]
        t_mat = [eye + x for x in pw]
        yield
        for _ in range(int(math.log2(c)) - 1):
            pw16 = [x.astype(BF16) for x in pw]
            pw = [_dot(x, x) for x in pw16]
            t_mat = [tm + _dot(tm.astype(BF16), x.astype(BF16)) for tm, x in zip(t_mat, pw)]
            yield
        t16 = [tm.astype(BF16) for tm in t_mat]
        for hd in heads:
            pu_scr[slot, hd] = _dot(t16[hd], (vc[hd] * b_col[hd]).astype(BF16))
            pw_scr[slot, hd] = _dot(t16[hd], (kb[hd] * eg[hd]).astype(BF16)).astype(BF16)
            pqk_scr[slot, hd] = (_dot_nt(qc[hd].astype(BF16), kc16[hd]) * decay[hd]).astype(BF16)
            pqd_scr[slot, hd] = (qc[hd] * eg[hd]).astype(BF16)
            pkd_scr[slot, hd] = (kc[hd] * jnp.exp(jnp.where(upper_rows, g_last[hd][0], g_last[hd][1])
                                                  - g_col[hd])).astype(BF16)
            for s in range(2):
                pgl_scr[slot, hd, s] = jnp.broadcast_to(jnp.exp(g_last[hd][s]), (8, LANES))
        yield

    def advance(p, slot):
        r0 = pl.multiple_of(p * pair, pair)
        zeros_c = jnp.zeros((c, GDN_DV), BF16)
        for s in range(2):
            rs = slice(s * c, (s + 1) * c)
            s_old = [s_scr[hd] for hd in heads]
            s16 = [x.astype(BF16) for x in s_old]
            v16 = [(pu_scr[slot, hd, rs, :] - _dot(pw_scr[slot, hd, rs, :], s16[hd])).astype(BF16) for hd in heads]
            yield
            vpad = [jnp.concatenate([v16[hd], zeros_c] if s == 0 else [zeros_c, v16[hd]], axis=0) for hd in heads]
            o = [_dot(pqd_scr[slot, hd, rs, :], s16[hd]) + _dot(pqk_scr[slot, hd, rs, :], vpad[hd]) for hd in heads]
            for ch, (bb, hd) in enumerate(chains):
                s_scr[ch] = s_old[ch] * pgl_scr[slot, ch, s, 0:1, :] + _dot_tn(pkd_scr[slot, ch, rs, :], v16[ch])
                o_scr[bb, pl.ds(r0 + s * c, c), hd * GDN_DV:(hd + 1) * GDN_DV] = o[ch]
            yield

    for _ in prepare(0, 0):
        pass

    def pair_body(p, carry):
        nxt = jnp.minimum(p + 1, n_pairs - 1)
        for _ in itertools.zip_longest(prepare(nxt, (p + 1) % 2), advance(p, p % 2)):
            pass
        return carry

    lax.fori_loop(0, n_pairs, pair_body, 0)

    for bb, hd in chains:
        cols = slice(hd * GDN_DV, (hd + 1) * GDN_DV)
        o = o_scr[bb, :, cols]
        ms = jnp.mean(o * o, axis=-1, keepdims=True)
        y = o * lax.rsqrt(ms + NORM_EPS) * nw_ref[...]
        y_ref[bb, :, cols] = (y * _silu(z_ref[bb, :, cols])).astype(BF16)


def _gdn(qkv, ab, z, conv_w, alog_row, dtb_row, norm_row, pad, tg, nb):
    b, lp, _ = qkv.shape
    grid = (b // nb, lp // tg)
    tile = lambda n: pl.BlockSpec((nb, tg, n), lambda bi, i: (bi, i, 0))
    kern = functools.partial(_gdn_kernel, pad=pad, tg=tg, nb=nb)
    nch = nb * GDN_HEADS
    return pl.pallas_call(
        kern,
        grid=grid,
        in_specs=[tile(GDN_CONV_CH), tile(LANES), tile(D_GDN), _const_spec(conv_w.shape),
                  _const_spec((1, LANES)), _const_spec((1, LANES)), _const_spec((1, GDN_DV))],
        out_specs=tile(D_GDN),
        out_shape=jax.ShapeDtypeStruct((b, lp, D_GDN), BF16),
        scratch_shapes=[pltpu.VMEM((nb, tg + HALO, GDN_CONV_CH), F32),
                        pltpu.VMEM((nb, tg, GDN_CONV_CH), F32),
                        pltpu.VMEM((nb, tg, LANES), F32),
                        pltpu.VMEM((nb, tg, LANES), F32),
                        pltpu.VMEM((nb, tg, D_GDN), F32),
                        pltpu.VMEM((nch, GDN_DK, GDN_DV), F32),
                        pltpu.VMEM((2, nch, 2 * GDN_CHUNK, GDN_DV), F32),
                        pltpu.VMEM((2, nch, 2 * GDN_CHUNK, GDN_DK), BF16),
                        pltpu.VMEM((2, nch, 2 * GDN_CHUNK, 2 * GDN_CHUNK), BF16),
                        pltpu.VMEM((2, nch, 2 * GDN_CHUNK, GDN_DK), BF16),
                        pltpu.VMEM((2, nch, 2 * GDN_CHUNK, GDN_DK), BF16),
                        pltpu.VMEM((2, nch, 2, 8, LANES), F32)],
        compiler_params=pltpu.CompilerParams(dimension_semantics=("arbitrary", "arbitrary"),
                                             vmem_limit_bytes=VMEM_LIMIT),
        name="gdn",
    )(qkv, ab, z, conv_w, alog_row, dtb_row, norm_row)


def _sb_kernel(q_ref, k_ref, v_ref, nw_ref, tri_ref, y_ref, qm_scr, acc_scr, run_scr, *, pad):
    i = pl.program_id(1)
    t = SB_TILE
    nh = SB_HEADS
    lane = lax.broadcasted_iota(jnp.int32, (t, LANES), 1)
    first = lane < SB_DH
    pair_cols = lambda hd: slice((hd // 2) * LANES, (hd // 2 + 1) * LANES)
    for hd in range(nh):
        qp = q_ref[0, :, pair_cols(hd)]
        zero = jnp.zeros_like(qp)
        qm_scr[hd] = jnp.where(first, qp, zero) if hd % 2 == 0 else jnp.where(first, zero, qp)
    tri = tri_ref[...]
    acc_scr[...] = jnp.zeros_like(acc_scr)
    run_scr[...] = jnp.zeros_like(run_scr)
    qi = lax.broadcasted_iota(jnp.int32, (t, t), 0)
    ki = lax.broadcasted_iota(jnp.int32, (t, t), 1)

    def blocks(js, mask):
        items = [(pl.ds(pl.multiple_of(j * t, t), t), hd) for j in js for hd in range(nh)]
        n = len(items)
        z, sp16, cs, att = ({} for _ in range(4))
        for step in range(n + 4):
            it = step
            if it < n:
                rows, hd = items[it]
                z[it] = _dot_nt(qm_scr[hd], k_ref[0, rows, pair_cols(hd)])
            it = step - 1
            if 0 <= it < n:
                zb = z[it].astype(BF16)
                sp = jnp.maximum(zb, 0.0) + jnp.log(1.0 + jnp.exp2(jnp.abs(zb) * (-LOG2E)))
                if mask is not None:
                    sp = jnp.where(mask, sp, jnp.zeros_like(sp))
                sp16[it] = sp
            it = step - 2
            if 0 <= it < n:
                cs[it] = _dot(sp16[it], tri)
            it = step - 3
            if 0 <= it < n:
                hd = items[it][1]
                run = run_scr[hd]
                e = (z[it] - cs[it]) - jnp.concatenate([run, run], axis=1)
                if mask is not None:
                    e = jnp.where(mask, e, NEG_BIG)
                att[it] = jnp.exp(e).astype(BF16)
                run_scr[hd] = run + cs[it][:, 0:1]
            it = step - 4
            if 0 <= it < n:
                rows, hd = items[it]
                acc_scr[hd] += _dot(att[it], v_ref[0, rows, pair_cols(hd)])

    blocks([i], (ki < qi) & (ki + i * t >= pad))

    n_inner = jnp.maximum(i - 1, 0)

    def inner_quad(n, carry):
        top = i - 1 - 4 * n
        blocks([top, top - 1, top - 2, top - 3], None)
        return carry

    lax.fori_loop(0, n_inner // 4, inner_quad, 0)
    rem = n_inner % 4

    @pl.when(rem >= 2)
    def _():
        blocks([rem, rem - 1], None)

    @pl.when(rem % 2 == 1)
    def _():
        blocks([1], None)

    @pl.when(i >= 1)
    def _():
        blocks([0], ki >= pad)

    for pr in range(nh // 2):
        o = jnp.where(first, acc_scr[2 * pr], acc_scr[2 * pr + 1])
        sq = o * o
        ss_first = jnp.sum(jnp.where(first, sq, 0.0), axis=-1, keepdims=True)
        ss_all = jnp.sum(sq, axis=-1, keepdims=True)
        ms = jnp.where(first, ss_first, ss_all - ss_first) * (1.0 / SB_DH)
        cols = slice(pr * LANES, (pr + 1) * LANES)
        y_ref[0, :, cols] = (o * lax.rsqrt(ms + NORM_EPS) * nw_ref[:, cols]).astype(BF16)


def _sb(q, k, v, norm_row, tri, pad):
    b, lp, _ = q.shape
    t = SB_TILE
    kern = functools.partial(_sb_kernel, pad=pad)
    full = pl.BlockSpec((1, lp, D_SB), lambda bi, i: (bi, 0, 0), pipeline_mode=pl.Buffered(1))
    tile = pl.BlockSpec((1, t, D_SB), lambda bi, i: (bi, i, 0))
    return pl.pallas_call(
        kern,
        grid=(b, lp // t),
        in_specs=[tile, full, full, _const_spec((1, D_SB)), _const_spec((t, t))],
        out_specs=tile,
        out_shape=jax.ShapeDtypeStruct((b, lp, D_SB), BF16),
        scratch_shapes=[pltpu.VMEM((SB_HEADS, t, LANES), BF16),
                        pltpu.VMEM((SB_HEADS, t, LANES), F32),
                        pltpu.VMEM((SB_HEADS, t, LANES), F32)],
        compiler_params=pltpu.CompilerParams(dimension_semantics=("parallel", "arbitrary"),
                                             vmem_limit_bytes=VMEM_LIMIT),
        name="sb",
    )(q, k, v, norm_row, tri)


def _out_proj_kernel(*refs, nsub, tiles_per_batch, use_head):
    src_refs, (head_ref, yg_ref, ys_ref, wg_ref, ws_ref, post_ref, pre_ref, h1_ref, u2_ref,
               h_scr) = refs[:nsub], refs[nsub:]
    _gather_rows(src_refs, head_ref, h_scr, pl.program_id(0) % tiles_per_batch == 0, use_head)
    mix = _dot(yg_ref[...], wg_ref[...]) + _dot(ys_ref[...], ws_ref[...])
    ms = jnp.mean(mix * mix, axis=-1, keepdims=True)
    h1 = h_scr[...] + mix * lax.rsqrt(ms + NORM_EPS) * post_ref[...]
    h1_ref[...] = h1
    ms1 = jnp.mean(h1 * h1, axis=-1, keepdims=True)
    u2_ref[...] = (h1 * lax.rsqrt(ms1 + NORM_EPS) * pre_ref[...]).astype(BF16)


def _out_proj(yg, ys, source, wg, ws, post_w, pre_w, tm, tiles_per_batch):
    src, maps, head, use_head = source
    rows, d = yg.shape[0], src.shape[-1]
    row_spec = lambda n: pl.BlockSpec((tm, n), lambda i: (i, 0))
    kern = functools.partial(_out_proj_kernel, nsub=len(maps), tiles_per_batch=tiles_per_batch, use_head=use_head)
    return pl.pallas_call(
        kern,
        grid=(rows // tm,),
        in_specs=[pl.BlockSpec((ROW_ALIGN, d), m) for m in maps]
                 + [_const_spec(head.shape), row_spec(D_GDN), row_spec(D_SB), _const_spec(wg.shape),
                    _const_spec(ws.shape), _const_spec((1, d)), _const_spec((1, d))],
        out_specs=[row_spec(d), row_spec(d)],
        out_shape=[jax.ShapeDtypeStruct((rows, d), F32), jax.ShapeDtypeStruct((rows, d), BF16)],
        scratch_shapes=[pltpu.VMEM((tm, d), F32)],
        compiler_params=pltpu.CompilerParams(dimension_semantics=("parallel",), vmem_limit_bytes=VMEM_LIMIT),
        name="out_proj",
    )(*([src] * len(maps)), head, yg, ys, wg, ws, post_w, pre_w)


def _ffn_kernel(u_ref, halo_ref, h1_ref, wup_ref, cw_ref, cb_ref, wd_ref, post_ref, out_ref, uext, acc,
                *, tm, tiles_per_batch, d_ff, fc, pad, final):
    i = pl.program_id(0)
    if final:
        uext[0:HALO, :] = halo_ref[...]
    else:
        keep = jnp.where(i % tiles_per_batch == 0, 0.0, 1.0).astype(BF16)
        uext[0:HALO, :] = halo_ref[...] * keep
    uext[HALO:, :] = u_ref[...]
    ue = uext[...]

    def conv(hx, cols):
        y = hx * cw_ref[FFN_CONV - 1:FFN_CONV, cols]
        for sh in range(1, FFN_CONV):
            y = y + pltpu.roll(hx, sh, axis=0) * cw_ref[FFN_CONV - 1 - sh:FFN_CONV - sh, cols]
        return y[HALO:, :] + cb_ref[:, cols]

    n_chunks = d_ff // fc
    gcols = [slice(c * fc, (c + 1) * fc) for c in range(n_chunks)]
    ucols = [slice(d_ff + c * fc, d_ff + (c + 1) * fc) for c in range(n_chunks)]
    hg, hu, act = {}, {}, {}
    for step in range(n_chunks + 2):
        c = step
        if c < n_chunks:
            hg[c] = _dot(ue, wup_ref[:, gcols[c]])
            hu[c] = _dot(ue, wup_ref[:, ucols[c]])
        c = step - 1
        if 0 <= c < n_chunks:
            gate = conv(hg.pop(c), gcols[c])
            up = conv(hu.pop(c), ucols[c])
            act[c] = (jax.nn.gelu(gate, approximate=True) * up).astype(BF16)
        c = step - 2
        if 0 <= c < n_chunks:
            part = _dot(act.pop(c), wd_ref[gcols[c], :])
            if c == 0:
                acc[...] = part
            else:
                acc[...] += part

    f = acc[...]
    ms = jnp.mean(f * f, axis=-1, keepdims=True)
    out = h1_ref[...] + f * lax.rsqrt(ms + NORM_EPS) * post_ref[...]
    if not final:
        t = (i % tiles_per_batch) * tm + lax.broadcasted_iota(jnp.int32, out.shape, 0)
        out = jnp.where(t >= pad, out, 0.0)
    out_ref[...] = out


def _ffn(u2, h1, w_up, conv_w, conv_b, w_down, post_w, tm, lp, seq, fc, pad, final):
    rows, d = h1.shape
    b = rows // lp
    d_ff = w_down.shape[0]
    if final:
        tiles, out_rows = seq // tm, b * seq
        first = lambda g: (g // tiles) * (lp // tm) + (lp - seq) // tm + g % tiles
    else:
        tiles, out_rows = lp // tm, rows
        first = lambda g: g
    kern = functools.partial(_ffn_kernel, tm=tm, tiles_per_batch=tiles, d_ff=d_ff, fc=fc, pad=pad, final=final)
    in_spec = pl.BlockSpec((tm, d), lambda g: (first(g), 0))
    halo_spec = pl.BlockSpec((HALO, d), lambda g: (jnp.maximum(first(g) * (tm // HALO) - 1, 0), 0))
    return pl.pallas_call(
        kern,
        grid=(out_rows // tm,),
        in_specs=[in_spec, halo_spec, in_spec, _const_spec(w_up.shape), _const_spec(conv_w.shape),
                  _const_spec(conv_b.shape), _const_spec(w_down.shape), _const_spec((1, d))],
        out_specs=pl.BlockSpec((tm, d), lambda g: (g, 0)),
        out_shape=jax.ShapeDtypeStruct((out_rows, d), F32),
        scratch_shapes=[pltpu.VMEM((tm + HALO, d), BF16), pltpu.VMEM((tm, d), F32)],
        compiler_params=pltpu.CompilerParams(dimension_semantics=("parallel",), vmem_limit_bytes=VMEM_LIMIT),
        name="ffn",
    )(u2, u2, h1, w_up, conv_w, conv_b, w_down, post_w)


def _layer(hp, x, meta_tokens, pad, lp, last, attn_pre_norm, w_in, gdn_conv_w, gdn_A_log, gdn_dt_bias, gdn_norm_w,
           sb_norm_w, w_out, attn_post_norm, ffn_pre_norm, w_ffn_up, ffn_conv_w, ffn_conv_b, w_ffn_down,
           ffn_post_norm):
    b, _, d = x.shape
    rows = b * lp
    tm = _pick_tile(lp, (768, 512, 256))
    nb = 2 if b % 2 == 0 else 1
    tg = _pick_tile(lp, (768 // nb, 256))
    d_ff = w_ffn_down.shape[0]
    fc = _pick_tile(d_ff, (256, 128))

    o_a = 2 * GDN_QK + D_GDN
    o_z = o_a + 2 * GDN_HEADS
    w_main = jnp.concatenate([w_in[:, :o_a], w_in[:, o_z:]], axis=1).astype(BF16)
    w_ab = jnp.pad(w_in[:, o_a:o_z], ((0, 0), (0, LANES - 2 * GDN_HEADS))).astype(BF16)
    row = lambda v: v.reshape(1, -1).astype(F32)
    lane_row = lambda v: jnp.pad(v.astype(F32), (0, LANES - v.shape[0])).reshape(1, LANES)

    source = _row_source(None if hp is None else hp.reshape(rows, d), x, meta_tokens, pad, tm, lp)
    qkvg, z, qs, ks, vs, ab = _in_proj(source, rows, row(attn_pre_norm), w_main, w_ab, tm, lp // tm)
    r3 = lambda t: t.reshape(b, lp, t.shape[-1])
    yg = _gdn(r3(qkvg), r3(ab), r3(z), gdn_conv_w.astype(F32), lane_row(gdn_A_log), lane_row(gdn_dt_bias),
              row(gdn_norm_w), pad, tg, nb)
    ti = lax.broadcasted_iota(jnp.int32, (SB_TILE, SB_TILE), 0)
    tj = lax.broadcasted_iota(jnp.int32, (SB_TILE, SB_TILE), 1)
    tri = (ti >= tj).astype(BF16)
    ys = _sb(r3(qs), r3(ks), r3(vs), row(sb_norm_w), tri, pad)
    h1, u2 = _out_proj(yg.reshape(rows, D_GDN), ys.reshape(rows, D_SB), source, w_out[:D_GDN].astype(BF16),
                       w_out[D_GDN:].astype(BF16), row(attn_post_norm), row(ffn_pre_norm), tm, lp // tm)
    seq = x.shape[1]
    tf = ROW_ALIGN
    final = last and (pad + N_META) % tf == 0 and seq % tf == 0
    out = _ffn(u2, h1, w_ffn_up.astype(BF16), ffn_conv_w.astype(F32), row(ffn_conv_b), w_ffn_down.astype(BF16),
               row(ffn_post_norm), tf, lp, seq, fc, pad, final)
    return out.reshape(b, seq if final else lp, d)


def kernel(x, meta_tokens, attn_pre_norm, w_in, gdn_conv_w, gdn_A_log, gdn_dt_bias, gdn_norm_w, sb_norm_w, w_out,
           attn_post_norm, ffn_pre_norm, w_ffn_up, ffn_conv_w, ffn_conv_b, w_ffn_down, ffn_post_norm):
    b, seq, d = x.shape
    depth = w_in.shape[0]
    length = N_META + seq
    pad = (-length) % ROW_ALIGN
    lp = length + pad
    hp = None
    if pad + N_META != ROW_ALIGN:
        meta = jnp.broadcast_to(meta_tokens[None].astype(x.dtype), (b, N_META, d))
        hp = jnp.concatenate([jnp.zeros((b, pad, d), x.dtype), meta, x], axis=1)
    for l in range(depth):
        hp = _layer(hp, x, meta_tokens, pad, lp, l == depth - 1, attn_pre_norm[l], w_in[l], gdn_conv_w[l],
                    gdn_A_log[l], gdn_dt_bias[l], gdn_norm_w[l], sb_norm_w[l], w_out[l], attn_post_norm[l],
                    ffn_pre_norm[l], w_ffn_up[l], ffn_conv_w[l], ffn_conv_b[l], w_ffn_down[l], ffn_post_norm[l])
    return hp if hp.shape[1] == seq else hp[:, pad + N_META:]
```

```python
import functools
import itertools
import math

import jax
import jax.numpy as jnp
from jax import lax
from jax.experimental import pallas as pl
from jax.experimental.pallas import tpu as pltpu

F32 = jnp.float32
BF16 = jnp.bfloat16

N_META = 16
GDN_HEADS = 4
GDN_DK = 128
GDN_DV = 128
GDN_CONV = 4
GDN_CHUNK = 64
GDN_QK = GDN_HEADS * GDN_DK
D_GDN = GDN_HEADS * GDN_DV
GDN_CONV_CH = 2 * GDN_QK + D_GDN
SB_HEADS = 8
SB_DH = 64
D_SB = SB_HEADS * SB_DH
FFN_CONV = 3
NORM_EPS = 1e-6
L2_EPS = 1e-6

LANES = 128
SB_TILE = 256
ROW_ALIGN = SB_TILE
HALO = 16
CONV_STRIDES = (12, 4)
NEG_BIG = -1e30
LOG2E = 1.4426950408889634
VMEM_LIMIT = 56 * 1024 * 1024


def _pick_tile(n, candidates):
    for c in candidates:
        if n % c == 0:
            return c
    raise ValueError(f"no tile for {n}")


def _dot(a, b):
    return jnp.dot(a, b, preferred_element_type=F32)


def _dot_nt(a, b):
    return lax.dot_general(a, b, (((1,), (1,)), ((), ())), preferred_element_type=F32)


def _dot_tn(a, b):
    return lax.dot_general(a, b, (((0,), (0,)), ((), ())), preferred_element_type=F32)


def _split2(x):
    hi = x.astype(BF16)
    lo = (x - hi.astype(F32)).astype(BF16)
    return hi, lo


def _dot_split(a, b):
    ah, al = _split2(a)
    bh, bl = _split2(b)
    return _dot(ah, bh) + (_dot(ah, bl) + _dot(al, bh))


def _silu(x):
    return x * jax.nn.sigmoid(x)


def _softplus(x):
    return jnp.maximum(x, 0.0) + jnp.log(1.0 + jnp.exp(-jnp.abs(x)))


def _const_spec(shape):
    nd = len(shape)
    return pl.BlockSpec(shape, lambda *_: (0,) * nd, pipeline_mode=pl.Buffered(1))


def _gather_rows(src_refs, head_ref, h_scr, batch_start, use_head):
    for s, ref in enumerate(src_refs):
        h_scr[s * ROW_ALIGN:(s + 1) * ROW_ALIGN, :] = ref[...]
    if use_head:
        @pl.when(batch_start)
        def _():
            h_scr[0:ROW_ALIGN, :] = head_ref[...]


def _row_source(hp2, x, meta_tokens, pad, tm, lp):
    nsub = tm // ROW_ALIGN
    d = meta_tokens.shape[-1]
    if hp2 is not None:
        maps = [functools.partial(lambda s, g: (g * nsub + s, 0), s) for s in range(nsub)]
        return hp2, maps, jnp.zeros((ROW_ALIGN, d), F32), False
    b, seq, _ = x.shape
    tiles, src_blocks = lp // tm, seq // ROW_ALIGN
    maps = [functools.partial(
        lambda s, g: ((g // tiles) * src_blocks + jnp.maximum((g % tiles) * nsub + s - 1, 0), 0), s)
        for s in range(nsub)]
    head = jnp.concatenate([jnp.zeros((pad, d), F32), meta_tokens.astype(F32)], axis=0)
    return x.reshape(b * seq, d), maps, head, True


def _in_proj_kernel(*refs, nsub, tiles_per_batch, use_head):
    src_refs, (head_ref, nw_ref, w_ref, wab_ref, qkvg_ref, z_ref, qs_ref, ks_ref, vs_ref, ab_ref,
               h_scr, u_scr) = refs[:nsub], refs[nsub:]
    _gather_rows(src_refs, head_ref, h_scr, pl.program_id(0) % tiles_per_batch == 0, use_head)
    h = h_scr[...]
    ms = jnp.mean(h * h, axis=-1, keepdims=True)
    u_scr[...] = (h * lax.rsqrt(ms + NORM_EPS) * nw_ref[...]).astype(BF16)
    u = u_scr[...]
    c0 = 0
    qkvg_ref[...] = _dot(u, w_ref[:, c0:c0 + GDN_CONV_CH])
    c0 += GDN_CONV_CH
    z_ref[...] = _dot(u, w_ref[:, c0:c0 + D_GDN])
    c0 += D_GDN
    qs_ref[...] = (_dot(u, w_ref[:, c0:c0 + D_SB]) * (SB_DH ** -0.5)).astype(BF16)
    c0 += D_SB
    ks_ref[...] = _dot(u, w_ref[:, c0:c0 + D_SB]).astype(BF16)
    c0 += D_SB
    vs_ref[...] = _dot(u, w_ref[:, c0:c0 + D_SB]).astype(BF16)
    ab_ref[...] = _dot(u, wab_ref[...])


def _in_proj(source, rows, norm_w, w_main, w_ab, tm, tiles_per_batch):
    src, maps, head, use_head = source
    d = src.shape[-1]
    grid = (rows // tm,)
    row_spec = lambda n: pl.BlockSpec((tm, n), lambda i: (i, 0))
    kern = functools.partial(_in_proj_kernel, nsub=len(maps), tiles_per_batch=tiles_per_batch, use_head=use_head)
    return pl.pallas_call(
        kern,
        grid=grid,
        in_specs=[pl.BlockSpec((ROW_ALIGN, d), m) for m in maps]
                 + [_const_spec(head.shape), _const_spec((1, d)), _const_spec(w_main.shape), _const_spec(w_ab.shape)],
        out_specs=[row_spec(GDN_CONV_CH), row_spec(D_GDN), row_spec(D_SB), row_spec(D_SB), row_spec(D_SB),
                   row_spec(LANES)],
        out_shape=[jax.ShapeDtypeStruct((rows, GDN_CONV_CH), F32), jax.ShapeDtypeStruct((rows, D_GDN), F32),
                   jax.ShapeDtypeStruct((rows, D_SB), BF16), jax.ShapeDtypeStruct((rows, D_SB), BF16),
                   jax.ShapeDtypeStruct((rows, D_SB), BF16), jax.ShapeDtypeStruct((rows, LANES), F32)],
        scratch_shapes=[pltpu.VMEM((tm, d), F32), pltpu.VMEM((tm, d), BF16)],
        compiler_params=pltpu.CompilerParams(dimension_semantics=("parallel",), vmem_limit_bytes=VMEM_LIMIT),
        name="in_proj",
    )(*([src] * len(maps)), head, norm_w, w_main, w_ab)


def _gdn_kernel(qkv_ref, ab_ref, z_ref, cw_ref, alog_ref, dtb_ref, nw_ref, y_ref,
                xbuf, qkvc, g_scr, beta_scr, o_scr, s_scr, pu_scr, pw_scr, pqk_scr, pqd_scr, pkd_scr, pgl_scr,
                *, pad, tg, nb):
    i = pl.program_id(1)
    pair = 2 * GDN_CHUNK
    c = GDN_CHUNK
    n_grp = GDN_CONV_CH // LANES

    @pl.when(i == 0)
    def _():
        s_scr[...] = jnp.zeros_like(s_scr)
        xbuf[:, :, 0:HALO, :] = jnp.zeros((nb, n_grp, HALO, LANES), F32)

    rows = i * tg + lax.broadcasted_iota(jnp.int32, (tg, LANES), 0)
    valid = rows >= pad
    stride = _pick_tile(tg // 8, CONV_STRIDES)
    starts = [st * 8 * stride + a for st in range(tg // (8 * stride)) for a in range(stride)]
    for bb in range(nb):
        for grp in range(n_grp):
            cols = slice(grp * LANES, (grp + 1) * LANES)
            xbuf[bb, grp, HALO:HALO + tg, :] = qkv_ref[bb, :, cols]
        for grp in range(n_grp):
            cols = slice(grp * LANES, (grp + 1) * LANES)
            acc = None
            for sh in range(GDN_CONV):
                tap = jnp.concatenate([xbuf[bb, grp, pl.ds(HALO + r - sh, 8, stride=stride), :] for r in starts], axis=0)
                term = tap * cw_ref[GDN_CONV - 1 - sh:GDN_CONV - sh, cols]
                acc = term if acc is None else acc + term
            y = _silu(acc)
            if grp < 2 * GDN_HEADS:
                y = y * lax.rsqrt(jnp.sum(y * y, axis=-1, keepdims=True) + L2_EPS)
                if grp < GDN_HEADS:
                    y = y * (GDN_DK ** -0.5)
            for n, r in enumerate(starts):
                qkvc[bb, grp, pl.ds(r, 8, stride=stride), :] = y[n * 8:(n + 1) * 8, :]
            xbuf[bb, grp, 0:HALO, :] = xbuf[bb, grp, tg:tg + HALO, :]

        ab = ab_ref[bb]
        g = -jnp.exp(alog_ref[...]) * _softplus(ab + dtb_ref[...])
        g_scr[bb] = jnp.where(valid, g, 0.0)
        beta_scr[bb] = jnp.where(valid, jax.nn.sigmoid(pltpu.roll(ab, LANES - GDN_HEADS, axis=1)), 0.0)

    ri = lax.broadcasted_iota(jnp.int32, (pair, pair), 0)
    ci = lax.broadcasted_iota(jnp.int32, (pair, pair), 1)
    same_chunk = jnp.right_shift(ri, int(math.log2(c))) == jnp.right_shift(ci, int(math.log2(c)))
    incl = (ci <= ri) & same_chunk
    strict = (ci < ri) & same_chunk
    cum_mat = jnp.where(incl, 1.0, 0.0).astype(BF16)
    eye = jnp.where(ri == ci, 1.0, 0.0)
    upper_rows = lax.broadcasted_iota(jnp.int32, (pair, 1), 0) < c
    chains = [(bb, hd) for bb in range(nb) for hd in range(GDN_HEADS)]
    heads = range(len(chains))

    n_pairs = tg // pair

    def prepare(p, slot):
        r0 = pl.multiple_of(p * pair, pair)
        rows_p = pl.ds(r0, pair)
        gcs, gcts = [], []
        for bb in range(nb):
            gp = g_scr[bb, rows_p, :]
            g_hi = gp.astype(BF16)
            g_r1 = gp - g_hi.astype(F32)
            g_mid = g_r1.astype(BF16)
            g_lo = (g_r1 - g_mid.astype(F32)).astype(BF16)
            gc = _dot(cum_mat, g_hi) + (_dot(cum_mat, g_mid) + _dot(cum_mat, g_lo))
            gcs.append(gc)
            gcts.append(gc.T)
        g_col = [gcs[bb][:, hd:hd + 1] for bb, hd in chains]
        g_last = [[gcs[bb][s * c + c - 1:s * c + c, hd:hd + 1] for s in range(2)] for bb, hd in chains]
        b_col = [beta_scr[bb, rows_p, :][:, hd:hd + 1] for bb, hd in chains]
        qc = [qkvc[bb, hd, rows_p, :] for bb, hd in chains]
        kc = [qkvc[bb, GDN_HEADS + hd, rows_p, :] for bb, hd in chains]
        vc = [qkvc[bb, 2 * GDN_HEADS + hd, rows_p, :] for bb, hd in chains]
        decay = [jnp.exp(jnp.where(incl, g_col[ch] - gcts[bb][hd:hd + 1, :], NEG_BIG))
                 for ch, (bb, hd) in enumerate(chains)]
        eg = [jnp.exp(g_col[hd]) for hd in heads]
        kb = [kc[hd] * b_col[hd] for hd in heads]
        kc16 = [kc[hd].astype(BF16) for hd in heads]
        a_mat = [jnp.where(strict, _dot_nt(kb[hd].astype(BF16), kc16[hd]) * decay[hd], 0.0) for hd in heads]
        pw = [-a for a in a_mat]
        t_mat = [eye + x for x in pw]
        yield
        for _ in range(int(math.log2(c)) - 1):
            pw16 = [x.astype(BF16) for x in pw]
            pw = [_dot(x, x) for x in pw16]
            t_mat = [tm + _dot(tm.astype(BF16), x.astype(BF16)) for tm, x in zip(t_mat, pw)]
            yield
        t16 = [tm.astype(BF16) for tm in t_mat]
        for hd in heads:
            pu_scr[slot, hd] = _dot(t16[hd], (vc[hd] * b_col[hd]).astype(BF16))
            pw_scr[slot, hd] = _dot(t16[hd], (kb[hd] * eg[hd]).astype(BF16)).astype(BF16)
            pqk_scr[slot, hd] = (_dot_nt(qc[hd].astype(BF16), kc16[hd]) * decay[hd]).astype(BF16)
            pqd_scr[slot, hd] = (qc[hd] * eg[hd]).astype(BF16)
            pkd_scr[slot, hd] = (kc[hd] * jnp.exp(jnp.where(upper_rows, g_last[hd][0], g_last[hd][1])
                                                  - g_col[hd])).astype(BF16)
            for s in range(2):
                pgl_scr[slot, hd, s] = jnp.broadcast_to(jnp.exp(g_last[hd][s]), (8, LANES))
        yield

    def advance(p, slot):
        r0 = pl.multiple_of(p * pair, pair)
        zeros_c = jnp.zeros((c, GDN_DV), BF16)
        for s in range(2):
            rs = slice(s * c, (s + 1) * c)
            s_old = [s_scr[hd] for hd in heads]
            s16 = [x.astype(BF16) for x in s_old]
            v16 = [(pu_scr[slot, hd, rs, :] - _dot(pw_scr[slot, hd, rs, :], s16[hd])).astype(BF16) for hd in heads]
            yield
            vpad = [jnp.concatenate([v16[hd], zeros_c] if s == 0 else [zeros_c, v16[hd]], axis=0) for hd in heads]
            o = [_dot(pqd_scr[slot, hd, rs, :], s16[hd]) + _dot(pqk_scr[slot, hd, rs, :], vpad[hd]) for hd in heads]
            for ch, (bb, hd) in enumerate(chains):
                s_scr[ch] = s_old[ch] * pgl_scr[slot, ch, s, 0:1, :] + _dot_tn(pkd_scr[slot, ch, rs, :], v16[ch])
                o_scr[bb, pl.ds(r0 + s * c, c), hd * GDN_DV:(hd + 1) * GDN_DV] = o[ch]
            yield

    for _ in prepare(0, 0):
        pass

    def pair_body(p, carry):
        nxt = jnp.minimum(p + 1, n_pairs - 1)
        for _ in itertools.zip_longest(prepare(nxt, (p + 1) % 2), advance(p, p % 2)):
            pass
        return carry

    lax.fori_loop(0, n_pairs, pair_body, 0)

    for bb, hd in chains:
        cols = slice(hd * GDN_DV, (hd + 1) * GDN_DV)
        o = o_scr[bb, :, cols]
        ms = jnp.mean(o * o, axis=-1, keepdims=True)
        y = o * lax.rsqrt(ms + NORM_EPS) * nw_ref[...]
        y_ref[bb, :, cols] = (y * _silu(z_ref[bb, :, cols])).astype(BF16)


def _gdn(qkv, ab, z, conv_w, alog_row, dtb_row, norm_row, pad, tg, nb):
    b, lp, _ = qkv.shape
    grid = (b // nb, lp // tg)
    tile = lambda n: pl.BlockSpec((nb, tg, n), lambda bi, i: (bi, i, 0))
    kern = functools.partial(_gdn_kernel, pad=pad, tg=tg, nb=nb)
    nch = nb * GDN_HEADS
    return pl.pallas_call(
        kern,
        grid=grid,
        in_specs=[tile(GDN_CONV_CH), tile(LANES), tile(D_GDN), _const_spec(conv_w.shape),
                  _const_spec((1, LANES)), _const_spec((1, LANES)), _const_spec((1, GDN_DV))],
        out_specs=tile(D_GDN),
        out_shape=jax.ShapeDtypeStruct((b, lp, D_GDN), BF16),
        scratch_shapes=[pltpu.VMEM((nb, GDN_CONV_CH // LANES, tg + HALO, LANES), F32),
                        pltpu.VMEM((nb, GDN_CONV_CH // LANES, tg, LANES), F32),
                        pltpu.VMEM((nb, tg, LANES), F32),
                        pltpu.VMEM((nb, tg, LANES), F32),
                        pltpu.VMEM((nb, tg, D_GDN), F32),
                        pltpu.VMEM((nch, GDN_DK, GDN_DV), F32),
                        pltpu.VMEM((2, nch, 2 * GDN_CHUNK, GDN_DV), F32),
                        pltpu.VMEM((2, nch, 2 * GDN_CHUNK, GDN_DK), BF16),
                        pltpu.VMEM((2, nch, 2 * GDN_CHUNK, 2 * GDN_CHUNK), BF16),
                        pltpu.VMEM((2, nch, 2 * GDN_CHUNK, GDN_DK), BF16),
                        pltpu.VMEM((2, nch, 2 * GDN_CHUNK, GDN_DK), BF16),
                        pltpu.VMEM((2, nch, 2, 8, LANES), F32)],
        compiler_params=pltpu.CompilerParams(dimension_semantics=("arbitrary", "arbitrary"),
                                             vmem_limit_bytes=VMEM_LIMIT),
        name="gdn",
    )(qkv, ab, z, conv_w, alog_row, dtb_row, norm_row)


def _sb_kernel(q_ref, k_ref, v_ref, nw_ref, tri_ref, y_ref, qm_scr, acc_scr, run_scr, *, pad):
    i = pl.program_id(1)
    t = SB_TILE
    nh = SB_HEADS
    lane = lax.broadcasted_iota(jnp.int32, (t, LANES), 1)
    first = lane < SB_DH
    pair_cols = lambda hd: slice((hd // 2) * LANES, (hd // 2 + 1) * LANES)
    for hd in range(nh):
        qp = q_ref[0, :, pair_cols(hd)]
        zero = jnp.zeros_like(qp)
        qm_scr[hd] = jnp.where(first, qp, zero) if hd % 2 == 0 else jnp.where(first, zero, qp)
    tri = tri_ref[...]
    acc_scr[...] = jnp.zeros_like(acc_scr)
    run_scr[...] = jnp.zeros_like(run_scr)
    qi = lax.broadcasted_iota(jnp.int32, (t, t), 0)
    ki = lax.broadcasted_iota(jnp.int32, (t, t), 1)

    def blocks(js, mask):
        items = [(pl.ds(pl.multiple_of(j * t, t), t), hd) for j in js for hd in range(nh)]
        n = len(items)
        z, sp16, cs, att = ({} for _ in range(4))
        for step in range(n + 4):
            it = step
            if it < n:
                rows, hd = items[it]
                z[it] = _dot_nt(qm_scr[hd], k_ref[0, rows, pair_cols(hd)])
            it = step - 1
            if 0 <= it < n:
                zb = z[it].astype(BF16)
                sp = jnp.maximum(zb, 0.0) + jnp.log(1.0 + jnp.exp2(jnp.abs(zb) * (-LOG2E)))
                if mask is not None:
                    sp = jnp.where(mask, sp, jnp.zeros_like(sp))
                sp16[it] = sp
            it = step - 2
            if 0 <= it < n:
                cs[it] = _dot(sp16[it], tri)
            it = step - 3
            if 0 <= it < n:
                hd = items[it][1]
                run = run_scr[hd]
                e = (z[it] - cs[it]) - jnp.concatenate([run, run], axis=1)
                if mask is not None:
                    e = jnp.where(mask, e, NEG_BIG)
                att[it] = jnp.exp(e).astype(BF16)
                run_scr[hd] = run + cs[it][:, 0:1]
            it = step - 4
            if 0 <= it < n:
                rows, hd = items[it]
                acc_scr[hd] += _dot(att[it], v_ref[0, rows, pair_cols(hd)])

    blocks([i], (ki < qi) & (ki + i * t >= pad))

    n_inner = jnp.maximum(i - 1, 0)

    def inner_quad(n, carry):
        top = i - 1 - 4 * n
        blocks([top, top - 1, top - 2, top - 3], None)
        return carry

    lax.fori_loop(0, n_inner // 4, inner_quad, 0)
    rem = n_inner % 4

    @pl.when(rem >= 2)
    def _():
        blocks([rem, rem - 1], None)

    @pl.when(rem % 2 == 1)
    def _():
        blocks([1], None)

    @pl.when(i >= 1)
    def _():
        blocks([0], ki >= pad)

    for pr in range(nh // 2):
        o = jnp.where(first, acc_scr[2 * pr], acc_scr[2 * pr + 1])
        sq = o * o
        ss_first = jnp.sum(jnp.where(first, sq, 0.0), axis=-1, keepdims=True)
        ss_all = jnp.sum(sq, axis=-1, keepdims=True)
        ms = jnp.where(first, ss_first, ss_all - ss_first) * (1.0 / SB_DH)
        cols = slice(pr * LANES, (pr + 1) * LANES)
        y_ref[0, :, cols] = (o * lax.rsqrt(ms + NORM_EPS) * nw_ref[:, cols]).astype(BF16)


def _sb(q, k, v, norm_row, tri, pad):
    b, lp, _ = q.shape
    t = SB_TILE
    kern = functools.partial(_sb_kernel, pad=pad)
    full = pl.BlockSpec((1, lp, D_SB), lambda bi, i: (bi, 0, 0), pipeline_mode=pl.Buffered(1))
    tile = pl.BlockSpec((1, t, D_SB), lambda bi, i: (bi, i, 0))
    return pl.pallas_call(
        kern,
        grid=(b, lp // t),
        in_specs=[tile, full, full, _const_spec((1, D_SB)), _const_spec((t, t))],
        out_specs=tile,
        out_shape=jax.ShapeDtypeStruct((b, lp, D_SB), BF16),
        scratch_shapes=[pltpu.VMEM((SB_HEADS, t, LANES), BF16),
                        pltpu.VMEM((SB_HEADS, t, LANES), F32),
                        pltpu.VMEM((SB_HEADS, t, LANES), F32)],
        compiler_params=pltpu.CompilerParams(dimension_semantics=("parallel", "arbitrary"),
                                             vmem_limit_bytes=VMEM_LIMIT),
        name="sb",
    )(q, k, v, norm_row, tri)


def _out_proj_kernel(*refs, nsub, tiles_per_batch, use_head):
    src_refs, (head_ref, yg_ref, ys_ref, wg_ref, ws_ref, post_ref, pre_ref, h1_ref, u2_ref,
               h_scr) = refs[:nsub], refs[nsub:]
    _gather_rows(src_refs, head_ref, h_scr, pl.program_id(0) % tiles_per_batch == 0, use_head)
    mix = _dot(yg_ref[...], wg_ref[...]) + _dot(ys_ref[...], ws_ref[...])
    ms = jnp.mean(mix * mix, axis=-1, keepdims=True)
    h1 = h_scr[...] + mix * lax.rsqrt(ms + NORM_EPS) * post_ref[...]
    h1_ref[...] = h1
    ms1 = jnp.mean(h1 * h1, axis=-1, keepdims=True)
    u2_ref[...] = (h1 * lax.rsqrt(ms1 + NORM_EPS) * pre_ref[...]).astype(BF16)


def _out_proj(yg, ys, source, wg, ws, post_w, pre_w, tm, tiles_per_batch):
    src, maps, head, use_head = source
    rows, d = yg.shape[0], src.shape[-1]
    row_spec = lambda n: pl.BlockSpec((tm, n), lambda i: (i, 0))
    kern = functools.partial(_out_proj_kernel, nsub=len(maps), tiles_per_batch=tiles_per_batch, use_head=use_head)
    return pl.pallas_call(
        kern,
        grid=(rows // tm,),
        in_specs=[pl.BlockSpec((ROW_ALIGN, d), m) for m in maps]
                 + [_const_spec(head.shape), row_spec(D_GDN), row_spec(D_SB), _const_spec(wg.shape),
                    _const_spec(ws.shape), _const_spec((1, d)), _const_spec((1, d))],
        out_specs=[row_spec(d), row_spec(d)],
        out_shape=[jax.ShapeDtypeStruct((rows, d), F32), jax.ShapeDtypeStruct((rows, d), BF16)],
        scratch_shapes=[pltpu.VMEM((tm, d), F32)],
        compiler_params=pltpu.CompilerParams(dimension_semantics=("parallel",), vmem_limit_bytes=VMEM_LIMIT),
        name="out_proj",
    )(*([src] * len(maps)), head, yg, ys, wg, ws, post_w, pre_w)


def _ffn_kernel(u_ref, halo_ref, h1_ref, wup_ref, cw_ref, cb_ref, wd_ref, post_ref, out_ref, uext, acc,
                *, tm, tiles_per_batch, d_ff, fc, pad, final):
    i = pl.program_id(0)
    if final:
        uext[0:HALO, :] = halo_ref[...]
    else:
        keep = jnp.where(i % tiles_per_batch == 0, 0.0, 1.0).astype(BF16)
        uext[0:HALO, :] = halo_ref[...] * keep
    uext[HALO:, :] = u_ref[...]
    ue = uext[...]

    def conv(hx, cols):
        y = hx * cw_ref[FFN_CONV - 1:FFN_CONV, cols]
        for sh in range(1, FFN_CONV):
            y = y + pltpu.roll(hx, sh, axis=0) * cw_ref[FFN_CONV - 1 - sh:FFN_CONV - sh, cols]
        return y[HALO:, :] + cb_ref[:, cols]

    n_chunks = d_ff // fc
    gcols = [slice(c * fc, (c + 1) * fc) for c in range(n_chunks)]
    ucols = [slice(d_ff + c * fc, d_ff + (c + 1) * fc) for c in range(n_chunks)]
    hg, hu, act = {}, {}, {}
    for step in range(n_chunks + 2):
        c = step
        if c < n_chunks:
            hg[c] = _dot(ue, wup_ref[:, gcols[c]])
            hu[c] = _dot(ue, wup_ref[:, ucols[c]])
        c = step - 1
        if 0 <= c < n_chunks:
            gate = conv(hg.pop(c), gcols[c])
            up = conv(hu.pop(c), ucols[c])
            act[c] = (jax.nn.gelu(gate, approximate=True) * up).astype(BF16)
        c = step - 2
        if 0 <= c < n_chunks:
            part = _dot(act.pop(c), wd_ref[gcols[c], :])
            if c == 0:
                acc[...] = part
            else:
                acc[...] += part

    f = acc[...]
    ms = jnp.mean(f * f, axis=-1, keepdims=True)
    out = h1_ref[...] + f * lax.rsqrt(ms + NORM_EPS) * post_ref[...]
    if not final:
        t = (i % tiles_per_batch) * tm + lax.broadcasted_iota(jnp.int32, out.shape, 0)
        out = jnp.where(t >= pad, out, 0.0)
    out_ref[...] = out


def _ffn(u2, h1, w_up, conv_w, conv_b, w_down, post_w, tm, lp, seq, fc, pad, final):
    rows, d = h1.shape
    b = rows // lp
    d_ff = w_down.shape[0]
    if final:
        tiles, out_rows = seq // tm, b * seq
        first = lambda g: (g // tiles) * (lp // tm) + (lp - seq) // tm + g % tiles
    else:
        tiles, out_rows = lp // tm, rows
        first = lambda g: g
    kern = functools.partial(_ffn_kernel, tm=tm, tiles_per_batch=tiles, d_ff=d_ff, fc=fc, pad=pad, final=final)
    in_spec = pl.BlockSpec((tm, d), lambda g: (first(g), 0))
    halo_spec = pl.BlockSpec((HALO, d), lambda g: (jnp.maximum(first(g) * (tm // HALO) - 1, 0), 0))
    return pl.pallas_call(
        kern,
        grid=(out_rows // tm,),
        in_specs=[in_spec, halo_spec, in_spec, _const_spec(w_up.shape), _const_spec(conv_w.shape),
                  _const_spec(conv_b.shape), _const_spec(w_down.shape), _const_spec((1, d))],
        out_specs=pl.BlockSpec((tm, d), lambda g: (g, 0)),
        out_shape=jax.ShapeDtypeStruct((out_rows, d), F32),
        scratch_shapes=[pltpu.VMEM((tm + HALO, d), BF16), pltpu.VMEM((tm, d), F32)],
        compiler_params=pltpu.CompilerParams(dimension_semantics=("parallel",), vmem_limit_bytes=VMEM_LIMIT),
        name="ffn",
    )(u2, u2, h1, w_up, conv_w, conv_b, w_down, post_w)


def _layer(hp, x, meta_tokens, pad, lp, last, attn_pre_norm, w_in, gdn_conv_w, gdn_A_log, gdn_dt_bias, gdn_norm_w,
           sb_norm_w, w_out, attn_post_norm, ffn_pre_norm, w_ffn_up, ffn_conv_w, ffn_conv_b, w_ffn_down,
           ffn_post_norm):
    b, _, d = x.shape
    rows = b * lp
    tm = _pick_tile(lp, (768, 512, 256))
    nb = 2 if b % 2 == 0 else 1
    tg = _pick_tile(lp, (768 // nb, 256))
    d_ff = w_ffn_down.shape[0]
    fc = _pick_tile(d_ff, (256, 128))

    o_a = 2 * GDN_QK + D_GDN
    o_z = o_a + 2 * GDN_HEADS
    w_main = jnp.concatenate([w_in[:, :o_a], w_in[:, o_z:]], axis=1).astype(BF16)
    w_ab = jnp.pad(w_in[:, o_a:o_z], ((0, 0), (0, LANES - 2 * GDN_HEADS))).astype(BF16)
    row = lambda v: v.reshape(1, -1).astype(F32)
    lane_row = lambda v: jnp.pad(v.astype(F32), (0, LANES - v.shape[0])).reshape(1, LANES)

    source = _row_source(None if hp is None else hp.reshape(rows, d), x, meta_tokens, pad, tm, lp)
    qkvg, z, qs, ks, vs, ab = _in_proj(source, rows, row(attn_pre_norm), w_main, w_ab, tm, lp // tm)
    r3 = lambda t: t.reshape(b, lp, t.shape[-1])
    yg = _gdn(r3(qkvg), r3(ab), r3(z), gdn_conv_w.astype(F32), lane_row(gdn_A_log), lane_row(gdn_dt_bias),
              row(gdn_norm_w), pad, tg, nb)
    ti = lax.broadcasted_iota(jnp.int32, (SB_TILE, SB_TILE), 0)
    tj = lax.broadcasted_iota(jnp.int32, (SB_TILE, SB_TILE), 1)
    tri = (ti >= tj).astype(BF16)
    ys = _sb(r3(qs), r3(ks), r3(vs), row(sb_norm_w), tri, pad)
    h1, u2 = _out_proj(yg.reshape(rows, D_GDN), ys.reshape(rows, D_SB), source, w_out[:D_GDN].astype(BF16),
                       w_out[D_GDN:].astype(BF16), row(attn_post_norm), row(ffn_pre_norm), tm, lp // tm)
    seq = x.shape[1]
    tf = ROW_ALIGN
    final = last and (pad + N_META) % tf == 0 and seq % tf == 0
    out = _ffn(u2, h1, w_ffn_up.astype(BF16), ffn_conv_w.astype(F32), row(ffn_conv_b), w_ffn_down.astype(BF16),
               row(ffn_post_norm), tf, lp, seq, fc, pad, final)
    return out.reshape(b, seq if final else lp, d)


def kernel(x, meta_tokens, attn_pre_norm, w_in, gdn_conv_w, gdn_A_log, gdn_dt_bias, gdn_norm_w, sb_norm_w, w_out,
           attn_post_norm, ffn_pre_norm, w_ffn_up, ffn_conv_w, ffn_conv_b, w_ffn_down, ffn_post_norm):
    b, seq, d = x.shape
    depth = w_in.shape[0]
    length = N_META + seq
    pad = (-length) % ROW_ALIGN
    lp = length + pad
    hp = None
    if pad + N_META != ROW_ALIGN:
        meta = jnp.broadcast_to(meta_tokens[None].astype(x.dtype), (b, N_META, d))
        hp = jnp.concatenate([jnp.zeros((b, pad, d), x.dtype), meta, x], axis=1)
    for l in range(depth):
        hp = _layer(hp, x, meta_tokens, pad, lp, l == depth - 1, attn_pre_norm[l], w_in[l], gdn_conv_w[l],
                    gdn_A_log[l], gdn_dt_bias[l], gdn_norm_w[l], sb_norm_w[l], w_out[l], attn_post_norm[l],
                    ffn_pre_norm[l], w_ffn_up[l], ffn_conv_w[l], ffn_conv_b[l], w_ffn_down[l], ffn_post_norm[l])
    return hp if hp.shape[1] == seq else hp[:, pad + N_META:]
```

```python
import functools
import itertools
import math

import jax
import jax.numpy as jnp
from jax import lax
from jax.experimental import pallas as pl
from jax.experimental.pallas import tpu as pltpu

F32 = jnp.float32
BF16 = jnp.bfloat16

N_META = 16
GDN_HEADS = 4
GDN_DK = 128
GDN_DV = 128
GDN_CONV = 4
GDN_CHUNK = 64
GDN_QK = GDN_HEADS * GDN_DK
D_GDN = GDN_HEADS * GDN_DV
GDN_CONV_CH = 2 * GDN_QK + D_GDN
SB_HEADS = 8
SB_DH = 64
D_SB = SB_HEADS * SB_DH
FFN_CONV = 3
NORM_EPS = 1e-6
L2_EPS = 1e-6

LANES = 128
SB_TILE = 256
ROW_ALIGN = SB_TILE
HALO = 16
CONV_STRIDES = (12, 4)
NEG_BIG = -1e30
LOG2E = 1.4426950408889634
VMEM_LIMIT = 56 * 1024 * 1024


def _pick_tile(n, candidates):
    for c in candidates:
        if n % c == 0:
            return c
    raise ValueError(f"no tile for {n}")


def _dot(a, b):
    return jnp.dot(a, b, preferred_element_type=F32)


def _dot_nt(a, b):
    return lax.dot_general(a, b, (((1,), (1,)), ((), ())), preferred_element_type=F32)


def _dot_tn(a, b):
    return lax.dot_general(a, b, (((0,), (0,)), ((), ())), preferred_element_type=F32)


def _silu(x):
    return x * jax.nn.sigmoid(x)


def _softplus(x):
    return jnp.maximum(x, 0.0) + jnp.log(1.0 + jnp.exp(-jnp.abs(x)))


def _const_spec(shape):
    nd = len(shape)
    return pl.BlockSpec(shape, lambda *_: (0,) * nd, pipeline_mode=pl.Buffered(1))


def _gather_rows(src_refs, head_ref, h_scr, batch_start, use_head):
    for s, ref in enumerate(src_refs):
        h_scr[s * ROW_ALIGN:(s + 1) * ROW_ALIGN, :] = ref[...]
    if use_head:
        @pl.when(batch_start)
        def _():
            h_scr[0:ROW_ALIGN, :] = head_ref[...]


def _row_source(hp2, x, meta_tokens, pad, tm, lp):
    nsub = tm // ROW_ALIGN
    d = meta_tokens.shape[-1]
    if hp2 is not None:
        maps = [functools.partial(lambda s, g: (g * nsub + s, 0), s) for s in range(nsub)]
        return hp2, maps, jnp.zeros((ROW_ALIGN, d), F32), False
    b, seq, _ = x.shape
    tiles, src_blocks = lp // tm, seq // ROW_ALIGN
    maps = [functools.partial(
        lambda s, g: ((g // tiles) * src_blocks + jnp.maximum((g % tiles) * nsub + s - 1, 0), 0), s)
        for s in range(nsub)]
    head = jnp.concatenate([jnp.zeros((pad, d), F32), meta_tokens.astype(F32)], axis=0)
    return x.reshape(b * seq, d), maps, head, True


def _in_proj_kernel(*refs, nsub, tiles_per_batch, use_head):
    src_refs, (head_ref, nw_ref, w_ref, wab_ref, qkvg_ref, z_ref, qs_ref, ks_ref, vs_ref, ab_ref,
               h_scr, u_scr) = refs[:nsub], refs[nsub:]
    _gather_rows(src_refs, head_ref, h_scr, pl.program_id(0) % tiles_per_batch == 0, use_head)
    h = h_scr[...]
    ms = jnp.mean(h * h, axis=-1, keepdims=True)
    u_scr[...] = (h * lax.rsqrt(ms + NORM_EPS) * nw_ref[...]).astype(BF16)
    u = u_scr[...]
    c0 = 0
    qkvg_ref[...] = _dot(u, w_ref[:, c0:c0 + GDN_CONV_CH])
    c0 += GDN_CONV_CH
    z_ref[...] = _dot(u, w_ref[:, c0:c0 + D_GDN])
    c0 += D_GDN
    qs_ref[...] = (_dot(u, w_ref[:, c0:c0 + D_SB]) * (SB_DH ** -0.5)).astype(BF16)
    c0 += D_SB
    ks_ref[...] = _dot(u, w_ref[:, c0:c0 + D_SB]).astype(BF16)
    c0 += D_SB
    vs_ref[...] = _dot(u, w_ref[:, c0:c0 + D_SB]).astype(BF16)
    ab_ref[...] = _dot(u, wab_ref[...])


def _in_proj(source, rows, norm_w, w_main, w_ab, tm, tiles_per_batch):
    src, maps, head, use_head = source
    d = src.shape[-1]
    grid = (rows // tm,)
    row_spec = lambda n: pl.BlockSpec((tm, n), lambda i: (i, 0))
    kern = functools.partial(_in_proj_kernel, nsub=len(maps), tiles_per_batch=tiles_per_batch, use_head=use_head)
    return pl.pallas_call(
        kern,
        grid=grid,
        in_specs=[pl.BlockSpec((ROW_ALIGN, d), m) for m in maps]
                 + [_const_spec(head.shape), _const_spec((1, d)), _const_spec(w_main.shape), _const_spec(w_ab.shape)],
        out_specs=[row_spec(GDN_CONV_CH), row_spec(D_GDN), row_spec(D_SB), row_spec(D_SB), row_spec(D_SB),
                   row_spec(LANES)],
        out_shape=[jax.ShapeDtypeStruct((rows, GDN_CONV_CH), F32), jax.ShapeDtypeStruct((rows, D_GDN), F32),
                   jax.ShapeDtypeStruct((rows, D_SB), BF16), jax.ShapeDtypeStruct((rows, D_SB), BF16),
                   jax.ShapeDtypeStruct((rows, D_SB), BF16), jax.ShapeDtypeStruct((rows, LANES), F32)],
        scratch_shapes=[pltpu.VMEM((tm, d), F32), pltpu.VMEM((tm, d), BF16)],
        compiler_params=pltpu.CompilerParams(dimension_semantics=("parallel",), vmem_limit_bytes=VMEM_LIMIT),
        name="in_proj",
    )(*([src] * len(maps)), head, norm_w, w_main, w_ab)


def _gdn_kernel(qkv_ref, ab_ref, z_ref, cw_ref, alog_ref, dtb_ref, nw_ref, y_ref,
                xbuf, qkvc, g_scr, beta_scr, o_scr, s_scr, pu_scr, pw_scr, pqk_scr, pqd_scr, pkd_scr, pgl_scr,
                *, pad, tg, nb):
    i = pl.program_id(1)
    pair = 2 * GDN_CHUNK
    c = GDN_CHUNK
    n_grp = GDN_CONV_CH // LANES

    @pl.when(i == 0)
    def _():
        s_scr[...] = jnp.zeros_like(s_scr)
        xbuf[:, :, 0:HALO, :] = jnp.zeros((nb, n_grp, HALO, LANES), F32)

    rows = i * tg + lax.broadcasted_iota(jnp.int32, (tg, LANES), 0)
    valid = rows >= pad
    stride = _pick_tile(tg // 8, CONV_STRIDES)
    starts = [st * 8 * stride + a for st in range(tg // (8 * stride)) for a in range(stride)]
    for bb in range(nb):
        for grp in range(n_grp):
            cols = slice(grp * LANES, (grp + 1) * LANES)
            xbuf[bb, grp, HALO:HALO + tg, :] = qkv_ref[bb, :, cols]
        for grp in range(n_grp):
            cols = slice(grp * LANES, (grp + 1) * LANES)
            acc = None
            for sh in range(GDN_CONV):
                tap = jnp.concatenate([xbuf[bb, grp, pl.ds(HALO + r - sh, 8, stride=stride), :] for r in starts], axis=0)
                term = tap * cw_ref[GDN_CONV - 1 - sh:GDN_CONV - sh, cols]
                acc = term if acc is None else acc + term
            y = _silu(acc)
            if grp < 2 * GDN_HEADS:
                y = y * lax.rsqrt(jnp.sum(y * y, axis=-1, keepdims=True) + L2_EPS)
                if grp < GDN_HEADS:
                    y = y * (GDN_DK ** -0.5)
            for n, r in enumerate(starts):
                qkvc[bb, grp, pl.ds(r, 8, stride=stride), :] = y[n * 8:(n + 1) * 8, :]
            xbuf[bb, grp, 0:HALO, :] = xbuf[bb, grp, tg:tg + HALO, :]

        ab = ab_ref[bb]
        g = -jnp.exp(alog_ref[...]) * _softplus(ab + dtb_ref[...])
        g_scr[bb] = jnp.where(valid, g, 0.0)
        beta_scr[bb] = jnp.where(valid, jax.nn.sigmoid(pltpu.roll(ab, LANES - GDN_HEADS, axis=1)), 0.0)

    ri = lax.broadcasted_iota(jnp.int32, (pair, pair), 0)
    ci = lax.broadcasted_iota(jnp.int32, (pair, pair), 1)
    same_chunk = jnp.right_shift(ri, int(math.log2(c))) == jnp.right_shift(ci, int(math.log2(c)))
    incl = (ci <= ri) & same_chunk
    strict = (ci < ri) & same_chunk
    cum_mat = jnp.where(incl, 1.0, 0.0).astype(BF16)
    eye = jnp.where(ri == ci, 1.0, 0.0)
    upper_rows = lax.broadcasted_iota(jnp.int32, (pair, 1), 0) < c
    chains = [(bb, hd) for bb in range(nb) for hd in range(GDN_HEADS)]
    heads = range(len(chains))

    n_pairs = tg // pair

    def prepare(p, slot):
        r0 = pl.multiple_of(p * pair, pair)
        rows_p = pl.ds(r0, pair)
        gcs, gcts = [], []
        for bb in range(nb):
            gp = g_scr[bb, rows_p, :]
            g_hi = gp.astype(BF16)
            g_r1 = gp - g_hi.astype(F32)
            g_mid = g_r1.astype(BF16)
            g_lo = (g_r1 - g_mid.astype(F32)).astype(BF16)
            gc = _dot(cum_mat, g_hi) + (_dot(cum_mat, g_mid) + _dot(cum_mat, g_lo))
            gcs.append(gc)
            gcts.append(gc.T)
        g_col = [gcs[bb][:, hd:hd + 1] for bb, hd in chains]
        g_last = [[gcs[bb][s * c + c - 1:s * c + c, hd:hd + 1] for s in range(2)] for bb, hd in chains]
        b_col = [beta_scr[bb, rows_p, :][:, hd:hd + 1] for bb, hd in chains]
        qc = [qkvc[bb, hd, rows_p, :] for bb, hd in chains]
        kc = [qkvc[bb, GDN_HEADS + hd, rows_p, :] for bb, hd in chains]
        vc = [qkvc[bb, 2 * GDN_HEADS + hd, rows_p, :] for bb, hd in chains]
        decay = [jnp.exp(jnp.where(incl, g_col[ch] - gcts[bb][hd:hd + 1, :], NEG_BIG))
                 for ch, (bb, hd) in enumerate(chains)]
        eg = [jnp.exp(g_col[hd]) for hd in heads]
        kb = [kc[hd] * b_col[hd] for hd in heads]
        kc16 = [kc[hd].astype(BF16) for hd in heads]
        a_mat = [jnp.where(strict, _dot_nt(kb[hd].astype(BF16), kc16[hd]) * decay[hd], 0.0) for hd in heads]
        pw = [-a for a in a_mat]
        t_mat = [eye + x for x in pw]
        yield
        for _ in range(int(math.log2(c)) - 1):
            pw16 = [x.astype(BF16) for x in pw]
            pw = [_dot(x, x) for x in pw16]
            t_mat = [tm + _dot(tm.astype(BF16), x.astype(BF16)) for tm, x in zip(t_mat, pw)]
            yield
        t16 = [tm.astype(BF16) for tm in t_mat]
        for hd in heads:
            pu_scr[slot, hd] = _dot(t16[hd], (vc[hd] * b_col[hd]).astype(BF16))
            pw_scr[slot, hd] = _dot(t16[hd], (kb[hd] * eg[hd]).astype(BF16)).astype(BF16)
            pqk_scr[slot, hd] = (_dot_nt(qc[hd].astype(BF16), kc16[hd]) * decay[hd]).astype(BF16)
            pqd_scr[slot, hd] = (qc[hd] * eg[hd]).astype(BF16)
            pkd_scr[slot, hd] = (kc[hd] * jnp.exp(jnp.where(upper_rows, g_last[hd][0], g_last[hd][1])
                                                  - g_col[hd])).astype(BF16)
            for s in range(2):
                pgl_scr[slot, hd, s] = jnp.broadcast_to(jnp.exp(g_last[hd][s]), (8, LANES))
        yield

    def advance(p, slot):
        r0 = pl.multiple_of(p * pair, pair)
        zeros_c = jnp.zeros((c, GDN_DV), BF16)
        for s in range(2):
            rs = slice(s * c, (s + 1) * c)
            s_old = [s_scr[hd] for hd in heads]
            s16 = [x.astype(BF16) for x in s_old]
            v16 = [(pu_scr[slot, hd, rs, :] - _dot(pw_scr[slot, hd, rs, :], s16[hd])).astype(BF16) for hd in heads]
            yield
            vpad = [jnp.concatenate([v16[hd], zeros_c] if s == 0 else [zeros_c, v16[hd]], axis=0) for hd in heads]
            o = [_dot(pqd_scr[slot, hd, rs, :], s16[hd]) + _dot(pqk_scr[slot, hd, rs, :], vpad[hd]) for hd in heads]
            for ch, (bb, hd) in enumerate(chains):
                s_scr[ch] = s_old[ch] * pgl_scr[slot, ch, s, 0:1, :] + _dot_tn(pkd_scr[slot, ch, rs, :], v16[ch])
                o_scr[bb, pl.ds(r0 + s * c, c), hd * GDN_DV:(hd + 1) * GDN_DV] = o[ch]
            yield

    for _ in prepare(0, 0):
        pass

    def pair_body(p, carry):
        nxt = jnp.minimum(p + 1, n_pairs - 1)
        for _ in itertools.zip_longest(prepare(nxt, (p + 1) % 2), advance(p, p % 2)):
            pass
        return carry

    lax.fori_loop(0, n_pairs, pair_body, 0)

    for bb, hd in chains:
        cols = slice(hd * GDN_DV, (hd + 1) * GDN_DV)
        o = o_scr[bb, :, cols]
        ms = jnp.mean(o * o, axis=-1, keepdims=True)
        y = o * lax.rsqrt(ms + NORM_EPS) * nw_ref[...]
        y_ref[bb, :, cols] = (y * _silu(z_ref[bb, :, cols])).astype(BF16)


def _gdn(qkv, ab, z, conv_w, alog_row, dtb_row, norm_row, pad, tg, nb):
    b, lp, _ = qkv.shape
    grid = (b // nb, lp // tg)
    tile = lambda n: pl.BlockSpec((nb, tg, n), lambda bi, i: (bi, i, 0))
    kern = functools.partial(_gdn_kernel, pad=pad, tg=tg, nb=nb)
    nch = nb * GDN_HEADS
    return pl.pallas_call(
        kern,
        grid=grid,
        in_specs=[tile(GDN_CONV_CH), tile(LANES), tile(D_GDN), _const_spec(conv_w.shape),
                  _const_spec((1, LANES)), _const_spec((1, LANES)), _const_spec((1, GDN_DV))],
        out_specs=tile(D_GDN),
        out_shape=jax.ShapeDtypeStruct((b, lp, D_GDN), BF16),
        scratch_shapes=[pltpu.VMEM((nb, GDN_CONV_CH // LANES, tg + HALO, LANES), F32),
                        pltpu.VMEM((nb, GDN_CONV_CH // LANES, tg, LANES), F32),
                        pltpu.VMEM((nb, tg, LANES), F32),
                        pltpu.VMEM((nb, tg, LANES), F32),
                        pltpu.VMEM((nb, tg, D_GDN), F32),
                        pltpu.VMEM((nch, GDN_DK, GDN_DV), F32),
                        pltpu.VMEM((2, nch, 2 * GDN_CHUNK, GDN_DV), F32),
                        pltpu.VMEM((2, nch, 2 * GDN_CHUNK, GDN_DK), BF16),
                        pltpu.VMEM((2, nch, 2 * GDN_CHUNK, 2 * GDN_CHUNK), BF16),
                        pltpu.VMEM((2, nch, 2 * GDN_CHUNK, GDN_DK), BF16),
                        pltpu.VMEM((2, nch, 2 * GDN_CHUNK, GDN_DK), BF16),
                        pltpu.VMEM((2, nch, 2, 8, LANES), F32)],
        compiler_params=pltpu.CompilerParams(dimension_semantics=("arbitrary", "arbitrary"),
                                             vmem_limit_bytes=VMEM_LIMIT),
        name="gdn",
    )(qkv, ab, z, conv_w, alog_row, dtb_row, norm_row)


def _sb_kernel(q_ref, k_ref, v_ref, nw_ref, tri_ref, y_ref, qm_scr, acc_scr, run_scr, *, pad):
    i = pl.program_id(1)
    t = SB_TILE
    nh = SB_HEADS
    lane = lax.broadcasted_iota(jnp.int32, (t, LANES), 1)
    first = lane < SB_DH
    pair_cols = lambda hd: slice((hd // 2) * LANES, (hd // 2 + 1) * LANES)
    for hd in range(nh):
        qp = q_ref[0, :, pair_cols(hd)]
        zero = jnp.zeros_like(qp)
        qm_scr[hd] = jnp.where(first, qp, zero) if hd % 2 == 0 else jnp.where(first, zero, qp)
    tri = tri_ref[...]
    acc_scr[...] = jnp.zeros_like(acc_scr)
    run_scr[...] = jnp.zeros_like(run_scr)
    qi = lax.broadcasted_iota(jnp.int32, (t, t), 0)
    ki = lax.broadcasted_iota(jnp.int32, (t, t), 1)

    def blocks(js, mask):
        items = [(pl.ds(pl.multiple_of(j * t, t), t), hd) for j in js for hd in range(nh)]
        n = len(items)
        z, sp16, cs, att = ({} for _ in range(4))
        for step in range(n + 4):
            it = step
            if it < n:
                rows, hd = items[it]
                z[it] = _dot_nt(qm_scr[hd], k_ref[0, rows, pair_cols(hd)])
            it = step - 1
            if 0 <= it < n:
                zb = z[it].astype(BF16)
                sp = jnp.maximum(zb, 0.0) + jnp.log(1.0 + jnp.exp2(jnp.abs(zb) * (-LOG2E)))
                if mask is not None:
                    sp = jnp.where(mask, sp, jnp.zeros_like(sp))
                sp16[it] = sp
            it = step - 2
            if 0 <= it < n:
                cs[it] = _dot(sp16[it], tri)
            it = step - 3
            if 0 <= it < n:
                hd = items[it][1]
                run = run_scr[hd]
                e = (z[it] - cs[it]) - jnp.concatenate([run, run], axis=1)
                if mask is not None:
                    e = jnp.where(mask, e, NEG_BIG)
                att[it] = jnp.exp(e).astype(BF16)
                run_scr[hd] = run + cs[it][:, 0:1]
            it = step - 4
            if 0 <= it < n:
                rows, hd = items[it]
                acc_scr[hd] += _dot(att[it], v_ref[0, rows, pair_cols(hd)])

    blocks([i], (ki < qi) & (ki + i * t >= pad))

    n_inner = jnp.maximum(i - 1, 0)

    def inner_quad(n, carry):
        top = i - 1 - 4 * n
        blocks([top, top - 1, top - 2, top - 3], None)
        return carry

    lax.fori_loop(0, n_inner // 4, inner_quad, 0)
    rem = n_inner % 4

    @pl.when(rem >= 2)
    def _():
        blocks([rem, rem - 1], None)

    @pl.when(rem % 2 == 1)
    def _():
        blocks([1], None)

    @pl.when(i >= 1)
    def _():
        blocks([0], ki >= pad)

    for pr in range(nh // 2):
        o = jnp.where(first, acc_scr[2 * pr], acc_scr[2 * pr + 1])
        sq = o * o
        ss_first = jnp.sum(jnp.where(first, sq, 0.0), axis=-1, keepdims=True)
        ss_all = jnp.sum(sq, axis=-1, keepdims=True)
        ms = jnp.where(first, ss_first, ss_all - ss_first) * (1.0 / SB_DH)
        cols = slice(pr * LANES, (pr + 1) * LANES)
        y_ref[0, :, cols] = (o * lax.rsqrt(ms + NORM_EPS) * nw_ref[:, cols]).astype(BF16)


def _sb(q, k, v, norm_row, tri, pad):
    b, lp, _ = q.shape
    t = SB_TILE
    kern = functools.partial(_sb_kernel, pad=pad)
    full = pl.BlockSpec((1, lp, D_SB), lambda bi, i: (bi, 0, 0), pipeline_mode=pl.Buffered(1))
    tile = pl.BlockSpec((1, t, D_SB), lambda bi, i: (bi, i, 0))
    return pl.pallas_call(
        kern,
        grid=(b, lp // t),
        in_specs=[tile, full, full, _const_spec((1, D_SB)), _const_spec((t, t))],
        out_specs=tile,
        out_shape=jax.ShapeDtypeStruct((b, lp, D_SB), BF16),
        scratch_shapes=[pltpu.VMEM((SB_HEADS, t, LANES), BF16),
                        pltpu.VMEM((SB_HEADS, t, LANES), F32),
                        pltpu.VMEM((SB_HEADS, t, LANES), F32)],
        compiler_params=pltpu.CompilerParams(dimension_semantics=("parallel", "arbitrary"),
                                             vmem_limit_bytes=VMEM_LIMIT),
        name="sb",
    )(q, k, v, norm_row, tri)


def _out_proj_kernel(*refs, nsub, tiles_per_batch, use_head):
    src_refs, (head_ref, yg_ref, ys_ref, wg_ref, ws_ref, post_ref, pre_ref, h1_ref, u2_ref,
               h_scr) = refs[:nsub], refs[nsub:]
    _gather_rows(src_refs, head_ref, h_scr, pl.program_id(0) % tiles_per_batch == 0, use_head)
    mix = _dot(yg_ref[...], wg_ref[...]) + _dot(ys_ref[...], ws_ref[...])
    ms = jnp.mean(mix * mix, axis=-1, keepdims=True)
    h1 = h_scr[...] + mix * lax.rsqrt(ms + NORM_EPS) * post_ref[...]
    h1_ref[...] = h1
    ms1 = jnp.mean(h1 * h1, axis=-1, keepdims=True)
    u2_ref[...] = (h1 * lax.rsqrt(ms1 + NORM_EPS) * pre_ref[...]).astype(BF16)


def _out_proj(yg, ys, source, wg, ws, post_w, pre_w, tm, tiles_per_batch):
    src, maps, head, use_head = source
    rows, d = yg.shape[0], src.shape[-1]
    row_spec = lambda n: pl.BlockSpec((tm, n), lambda i: (i, 0))
    kern = functools.partial(_out_proj_kernel, nsub=len(maps), tiles_per_batch=tiles_per_batch, use_head=use_head)
    return pl.pallas_call(
        kern,
        grid=(rows // tm,),
        in_specs=[pl.BlockSpec((ROW_ALIGN, d), m) for m in maps]
                 + [_const_spec(head.shape), row_spec(D_GDN), row_spec(D_SB), _const_spec(wg.shape),
                    _const_spec(ws.shape), _const_spec((1, d)), _const_spec((1, d))],
        out_specs=[row_spec(d), row_spec(d)],
        out_shape=[jax.ShapeDtypeStruct((rows, d), F32), jax.ShapeDtypeStruct((rows, d), BF16)],
        scratch_shapes=[pltpu.VMEM((tm, d), F32)],
        compiler_params=pltpu.CompilerParams(dimension_semantics=("parallel",), vmem_limit_bytes=VMEM_LIMIT),
        name="out_proj",
    )(*([src] * len(maps)), head, yg, ys, wg, ws, post_w, pre_w)


def _ffn_kernel(u_ref, halo_ref, h1_ref, wup_ref, cw_ref, cb_ref, wd_ref, post_ref, out_ref, uext, acc,
                *, tm, tiles_per_batch, d_ff, fc, pad, final):
    i = pl.program_id(0)
    if final:
        uext[0:HALO, :] = halo_ref[...]
    else:
        keep = jnp.where(i % tiles_per_batch == 0, 0.0, 1.0).astype(BF16)
        uext[0:HALO, :] = halo_ref[...] * keep
    uext[HALO:, :] = u_ref[...]
    ue = uext[...]

    def conv(hx, cols):
        y = hx * cw_ref[FFN_CONV - 1:FFN_CONV, cols]
        for sh in range(1, FFN_CONV):
            y = y + pltpu.roll(hx, sh, axis=0) * cw_ref[FFN_CONV - 1 - sh:FFN_CONV - sh, cols]
        return y[HALO:, :] + cb_ref[:, cols]

    n_chunks = d_ff // fc
    gcols = [slice(c * fc, (c + 1) * fc) for c in range(n_chunks)]
    ucols = [slice(d_ff + c * fc, d_ff + (c + 1) * fc) for c in range(n_chunks)]
    hg, hu, act = {}, {}, {}
    for step in range(n_chunks + 2):
        c = step
        if c < n_chunks:
            hg[c] = _dot(ue, wup_ref[:, gcols[c]])
            hu[c] = _dot(ue, wup_ref[:, ucols[c]])
        c = step - 1
        if 0 <= c < n_chunks:
            gate = conv(hg.pop(c), gcols[c])
            up = conv(hu.pop(c), ucols[c])
            act[c] = (jax.nn.gelu(gate, approximate=True) * up).astype(BF16)
        c = step - 2
        if 0 <= c < n_chunks:
            part = _dot(act.pop(c), wd_ref[gcols[c], :])
            if c == 0:
                acc[...] = part
            else:
                acc[...] += part

    f = acc[...]
    ms = jnp.mean(f * f, axis=-1, keepdims=True)
    out = h1_ref[...] + f * lax.rsqrt(ms + NORM_EPS) * post_ref[...]
    if not final:
        t = (i % tiles_per_batch) * tm + lax.broadcasted_iota(jnp.int32, out.shape, 0)
        out = jnp.where(t >= pad, out, 0.0)
    out_ref[...] = out


def _ffn(u2, h1, w_up, conv_w, conv_b, w_down, post_w, tm, lp, seq, fc, pad, final):
    rows, d = h1.shape
    b = rows // lp
    d_ff = w_down.shape[0]
    if final:
        tiles, out_rows = seq // tm, b * seq
        first = lambda g: (g // tiles) * (lp // tm) + (lp - seq) // tm + g % tiles
    else:
        tiles, out_rows = lp // tm, rows
        first = lambda g: g
    kern = functools.partial(_ffn_kernel, tm=tm, tiles_per_batch=tiles, d_ff=d_ff, fc=fc, pad=pad, final=final)
    in_spec = pl.BlockSpec((tm, d), lambda g: (first(g), 0))
    halo_spec = pl.BlockSpec((HALO, d), lambda g: (jnp.maximum(first(g) * (tm // HALO) - 1, 0), 0))
    return pl.pallas_call(
        kern,
        grid=(out_rows // tm,),
        in_specs=[in_spec, halo_spec, in_spec, _const_spec(w_up.shape), _const_spec(conv_w.shape),
                  _const_spec(conv_b.shape), _const_spec(w_down.shape), _const_spec((1, d))],
        out_specs=pl.BlockSpec((tm, d), lambda g: (g, 0)),
        out_shape=jax.ShapeDtypeStruct((out_rows, d), F32),
        scratch_shapes=[pltpu.VMEM((tm + HALO, d), BF16), pltpu.VMEM((tm, d), F32)],
        compiler_params=pltpu.CompilerParams(dimension_semantics=("parallel",), vmem_limit_bytes=VMEM_LIMIT),
        name="ffn",
    )(u2, u2, h1, w_up, conv_w, conv_b, w_down, post_w)


def _layer(hp, x, meta_tokens, pad, lp, last, attn_pre_norm, w_in, gdn_conv_w, gdn_A_log, gdn_dt_bias, gdn_norm_w,
           sb_norm_w, w_out, attn_post_norm, ffn_pre_norm, w_ffn_up, ffn_conv_w, ffn_conv_b, w_ffn_down,
           ffn_post_norm):
    b, _, d = x.shape
    rows = b * lp
    tm = _pick_tile(lp, (768, 512, 256))
    nb = 2 if b % 2 == 0 else 1
    tg = _pick_tile(lp, (768 // nb, 256))
    d_ff = w_ffn_down.shape[0]
    fc = _pick_tile(d_ff, (256, 128))

    o_a = 2 * GDN_QK + D_GDN
    o_z = o_a + 2 * GDN_HEADS
    w_main = jnp.concatenate([w_in[:, :o_a], w_in[:, o_z:]], axis=1).astype(BF16)
    w_ab = jnp.pad(w_in[:, o_a:o_z], ((0, 0), (0, LANES - 2 * GDN_HEADS))).astype(BF16)
    row = lambda v: v.reshape(1, -1).astype(F32)
    lane_row = lambda v: jnp.pad(v.astype(F32), (0, LANES - v.shape[0])).reshape(1, LANES)

    source = _row_source(None if hp is None else hp.reshape(rows, d), x, meta_tokens, pad, tm, lp)
    qkvg, z, qs, ks, vs, ab = _in_proj(source, rows, row(attn_pre_norm), w_main, w_ab, tm, lp // tm)
    r3 = lambda t: t.reshape(b, lp, t.shape[-1])
    yg = _gdn(r3(qkvg), r3(ab), r3(z), gdn_conv_w.astype(F32), lane_row(gdn_A_log), lane_row(gdn_dt_bias),
              row(gdn_norm_w), pad, tg, nb)
    ti = lax.broadcasted_iota(jnp.int32, (SB_TILE, SB_TILE), 0)
    tj = lax.broadcasted_iota(jnp.int32, (SB_TILE, SB_TILE), 1)
    tri = (ti >= tj).astype(BF16)
    ys = _sb(r3(qs), r3(ks), r3(vs), row(sb_norm_w), tri, pad)
    h1, u2 = _out_proj(yg.reshape(rows, D_GDN), ys.reshape(rows, D_SB), source, w_out[:D_GDN].astype(BF16),
                       w_out[D_GDN:].astype(BF16), row(attn_post_norm), row(ffn_pre_norm), tm, lp // tm)
    seq = x.shape[1]
    tf = ROW_ALIGN
    final = last and (pad + N_META) % tf == 0 and seq % tf == 0
    out = _ffn(u2, h1, w_ffn_up.astype(BF16), ffn_conv_w.astype(F32), row(ffn_conv_b), w_ffn_down.astype(BF16),
               row(ffn_post_norm), tf, lp, seq, fc, pad, final)
    return out.reshape(b, seq if final else lp, d)


def kernel(x, meta_tokens, attn_pre_norm, w_in, gdn_conv_w, gdn_A_log, gdn_dt_bias, gdn_norm_w, sb_norm_w, w_out,
           attn_post_norm, ffn_pre_norm, w_ffn_up, ffn_conv_w, ffn_conv_b, w_ffn_down, ffn_post_norm):
    b, seq, d = x.shape
    depth = w_in.shape[0]
    length = N_META + seq
    pad = (-length) % ROW_ALIGN
    lp = length + pad
    hp = None
    if pad + N_META != ROW_ALIGN:
        meta = jnp.broadcast_to(meta_tokens[None].astype(x.dtype), (b, N_META, d))
        hp = jnp.concatenate([jnp.zeros((b, pad, d), x.dtype), meta, x], axis=1)
    for l in range(depth):
        hp = _layer(hp, x, meta_tokens, pad, lp, l == depth - 1, attn_pre_norm[l], w_in[l], gdn_conv_w[l],
                    gdn_A_log[l], gdn_dt_bias[l], gdn_norm_w[l], sb_norm_w[l], w_out[l], attn_post_norm[l],
                    ffn_pre_norm[l], w_ffn_up[l], ffn_conv_w[l], ffn_conv_b[l], w_ffn_down[l], ffn_post_norm[l])
    return hp if hp.shape[1] == seq else hp[:, pad + N_META:]
```

```python
import functools
import itertools
import math

import jax
import jax.numpy as jnp
from jax import lax
from jax.experimental import pallas as pl
from jax.experimental.pallas import tpu as pltpu

F32 = jnp.float32
BF16 = jnp.bfloat16

N_META = 16
GDN_HEADS = 4
GDN_DK = 128
GDN_DV = 128
GDN_CONV = 4
GDN_CHUNK = 64
GDN_QK = GDN_HEADS * GDN_DK
D_GDN = GDN_HEADS * GDN_DV
GDN_CONV_CH = 2 * GDN_QK + D_GDN
SB_HEADS = 8
SB_DH = 64
D_SB = SB_HEADS * SB_DH
FFN_CONV = 3
NORM_EPS = 1e-6
L2_EPS = 1e-6

LANES = 128
SB_TILE = 256
ROW_ALIGN = SB_TILE
HALO = 16
CONV_STRIDES = (12, 4)
NEG_BIG = -1e30
LOG2E = 1.4426950408889634
VMEM_LIMIT = 56 * 1024 * 1024


def _pick_tile(n, candidates):
    for c in candidates:
        if n % c == 0:
            return c
    raise ValueError(f"no tile for {n}")


def _dot(a, b):
    return jnp.dot(a, b, preferred_element_type=F32)


def _dot_nt(a, b):
    return lax.dot_general(a, b, (((1,), (1,)), ((), ())), preferred_element_type=F32)


def _dot_tn(a, b):
    return lax.dot_general(a, b, (((0,), (0,)), ((), ())), preferred_element_type=F32)


def _silu(x):
    return x * jax.nn.sigmoid(x)


def _softplus(x):
    return jnp.maximum(x, 0.0) + jnp.log(1.0 + jnp.exp(-jnp.abs(x)))


def _const_spec(shape):
    nd = len(shape)
    return pl.BlockSpec(shape, lambda *_: (0,) * nd, pipeline_mode=pl.Buffered(1))


def _gather_rows(src_refs, head_ref, h_scr, batch_start, use_head):
    for s, ref in enumerate(src_refs):
        h_scr[s * ROW_ALIGN:(s + 1) * ROW_ALIGN, :] = ref[...]
    if use_head:
        @pl.when(batch_start)
        def _():
            h_scr[0:ROW_ALIGN, :] = head_ref[...]


def _row_source(hp2, x, meta_tokens, pad, tm, lp):
    nsub = tm // ROW_ALIGN
    d = meta_tokens.shape[-1]
    if hp2 is not None:
        maps = [functools.partial(lambda s, g: (g * nsub + s, 0), s) for s in range(nsub)]
        return hp2, maps, jnp.zeros((ROW_ALIGN, d), F32), False
    b, seq, _ = x.shape
    tiles, src_blocks = lp // tm, seq // ROW_ALIGN
    maps = [functools.partial(
        lambda s, g: ((g // tiles) * src_blocks + jnp.maximum((g % tiles) * nsub + s - 1, 0), 0), s)
        for s in range(nsub)]
    head = jnp.concatenate([jnp.zeros((pad, d), F32), meta_tokens.astype(F32)], axis=0)
    return x.reshape(b * seq, d), maps, head, True


def _in_proj_kernel(*refs, nsub, tiles_per_batch, use_head):
    src_refs, (head_ref, nw_ref, w_ref, wab_ref, qkvg_ref, z_ref, qs_ref, ks_ref, vs_ref, ab_ref,
               h_scr, u_scr) = refs[:nsub], refs[nsub:]
    _gather_rows(src_refs, head_ref, h_scr, pl.program_id(0) % tiles_per_batch == 0, use_head)
    h = h_scr[...]
    ms = jnp.mean(h * h, axis=-1, keepdims=True)
    u_scr[...] = (h * lax.rsqrt(ms + NORM_EPS) * nw_ref[...]).astype(BF16)
    u = u_scr[...]
    c0 = 0
    qkvg_ref[...] = _dot(u, w_ref[:, c0:c0 + GDN_CONV_CH])
    c0 += GDN_CONV_CH
    z_ref[...] = _dot(u, w_ref[:, c0:c0 + D_GDN])
    c0 += D_GDN
    qs_ref[...] = (_dot(u, w_ref[:, c0:c0 + D_SB]) * (SB_DH ** -0.5)).astype(BF16)
    c0 += D_SB
    ks_ref[...] = _dot(u, w_ref[:, c0:c0 + D_SB]).astype(BF16)
    c0 += D_SB
    vs_ref[...] = _dot(u, w_ref[:, c0:c0 + D_SB]).astype(BF16)
    ab_ref[...] = _dot(u, wab_ref[...])


def _in_proj(source, rows, norm_w, w_main, w_ab, tm, tiles_per_batch):
    src, maps, head, use_head = source
    d = src.shape[-1]
    grid = (rows // tm,)
    row_spec = lambda n: pl.BlockSpec((tm, n), lambda i: (i, 0))
    kern = functools.partial(_in_proj_kernel, nsub=len(maps), tiles_per_batch=tiles_per_batch, use_head=use_head)
    return pl.pallas_call(
        kern,
        grid=grid,
        in_specs=[pl.BlockSpec((ROW_ALIGN, d), m) for m in maps]
                 + [_const_spec(head.shape), _const_spec((1, d)), _const_spec(w_main.shape), _const_spec(w_ab.shape)],
        out_specs=[row_spec(GDN_CONV_CH), row_spec(D_GDN), row_spec(D_SB), row_spec(D_SB), row_spec(D_SB),
                   row_spec(LANES)],
        out_shape=[jax.ShapeDtypeStruct((rows, GDN_CONV_CH), F32), jax.ShapeDtypeStruct((rows, D_GDN), F32),
                   jax.ShapeDtypeStruct((rows, D_SB), BF16), jax.ShapeDtypeStruct((rows, D_SB), BF16),
                   jax.ShapeDtypeStruct((rows, D_SB), BF16), jax.ShapeDtypeStruct((rows, LANES), F32)],
        scratch_shapes=[pltpu.VMEM((tm, d), F32), pltpu.VMEM((tm, d), BF16)],
        compiler_params=pltpu.CompilerParams(dimension_semantics=("parallel",), vmem_limit_bytes=VMEM_LIMIT),
        name="in_proj",
    )(*([src] * len(maps)), head, norm_w, w_main, w_ab)


def _gdn_kernel(qkv_ref, ab_ref, z_ref, cw_ref, alog_ref, dtb_ref, nw_ref, y_ref,
                xbuf, qkvc, g_scr, beta_scr, o_scr, s_scr, pu_scr, pw_scr, pqk_scr, pqd_scr, pkd_scr, pgl_scr,
                *, pad, tg, nb):
    i = pl.program_id(1)
    pair = 2 * GDN_CHUNK
    c = GDN_CHUNK
    n_grp = GDN_CONV_CH // LANES

    @pl.when(i == 0)
    def _():
        s_scr[...] = jnp.zeros_like(s_scr)
        xbuf[:, :, 0:HALO, :] = jnp.zeros((nb, n_grp, HALO, LANES), F32)

    rows = i * tg + lax.broadcasted_iota(jnp.int32, (tg, LANES), 0)
    valid = rows >= pad
    stride = _pick_tile(tg // 8, CONV_STRIDES)
    starts = [st * 8 * stride + a for st in range(tg // (8 * stride)) for a in range(stride)]
    for bb in range(nb):
        for grp in range(n_grp):
            cols = slice(grp * LANES, (grp + 1) * LANES)
            xbuf[bb, grp, HALO:HALO + tg, :] = qkv_ref[bb, :, cols]
        for grp in range(n_grp):
            cols = slice(grp * LANES, (grp + 1) * LANES)
            acc = None
            for sh in range(GDN_CONV):
                tap = jnp.concatenate([xbuf[bb, grp, pl.ds(HALO + r - sh, 8, stride=stride), :] for r in starts], axis=0)
                term = tap * cw_ref[GDN_CONV - 1 - sh:GDN_CONV - sh, cols]
                acc = term if acc is None else acc + term
            y = _silu(acc)
            if grp < 2 * GDN_HEADS:
                y = y * lax.rsqrt(jnp.sum(y * y, axis=-1, keepdims=True) + L2_EPS)
                if grp < GDN_HEADS:
                    y = y * (GDN_DK ** -0.5)
            for n, r in enumerate(starts):
                qkvc[bb, grp, pl.ds(r, 8, stride=stride), :] = y[n * 8:(n + 1) * 8, :]
            xbuf[bb, grp, 0:HALO, :] = xbuf[bb, grp, tg:tg + HALO, :]

        ab = ab_ref[bb]
        g = -jnp.exp(alog_ref[...]) * _softplus(ab + dtb_ref[...])
        g_scr[bb] = jnp.where(valid, g, 0.0)
        beta_scr[bb] = jnp.where(valid, jax.nn.sigmoid(pltpu.roll(ab, LANES - GDN_HEADS, axis=1)), 0.0)

    ri = lax.broadcasted_iota(jnp.int32, (pair, pair), 0)
    ci = lax.broadcasted_iota(jnp.int32, (pair, pair), 1)
    same_chunk = jnp.right_shift(ri, int(math.log2(c))) == jnp.right_shift(ci, int(math.log2(c)))
    incl = (ci <= ri) & same_chunk
    strict = (ci < ri) & same_chunk
    cum_mat = jnp.where(incl, 1.0, 0.0).astype(BF16)
    eye = jnp.where(ri == ci, 1.0, 0.0)
    upper_rows = lax.broadcasted_iota(jnp.int32, (pair, 1), 0) < c
    chains = [(bb, hd) for bb in range(nb) for hd in range(GDN_HEADS)]
    heads = range(len(chains))

    n_pairs = tg // pair

    def prepare(p, slot):
        r0 = pl.multiple_of(p * pair, pair)
        rows_p = pl.ds(r0, pair)
        gcs, gcts = [], []
        for bb in range(nb):
            gp = g_scr[bb, rows_p, :]
            g_hi = gp.astype(BF16)
            g_r1 = gp - g_hi.astype(F32)
            g_mid = g_r1.astype(BF16)
            g_lo = (g_r1 - g_mid.astype(F32)).astype(BF16)
            gc = _dot(cum_mat, g_hi) + (_dot(cum_mat, g_mid) + _dot(cum_mat, g_lo))
            gcs.append(gc)
            gcts.append(gc.T)
        g_col = [gcs[bb][:, hd:hd + 1] for bb, hd in chains]
        g_last = [[gcs[bb][s * c + c - 1:s * c + c, hd:hd + 1] for s in range(2)] for bb, hd in chains]
        b_col = [beta_scr[bb, rows_p, :][:, hd:hd + 1] for bb, hd in chains]
        qc = [qkvc[bb, hd, rows_p, :] for bb, hd in chains]
        kc = [qkvc[bb, GDN_HEADS + hd, rows_p, :] for bb, hd in chains]
        vc = [qkvc[bb, 2 * GDN_HEADS + hd, rows_p, :] for bb, hd in chains]
        decay = [jnp.exp(jnp.where(incl, g_col[ch] - gcts[bb][hd:hd + 1, :], NEG_BIG))
                 for ch, (bb, hd) in enumerate(chains)]
        eg = [jnp.exp(g_col[hd]) for hd in heads]
        kb = [kc[hd] * b_col[hd] for hd in heads]
        kc16 = [kc[hd].astype(BF16) for hd in heads]
        a_mat = [jnp.where(strict, _dot_nt(kb[hd].astype(BF16), kc16[hd]) * decay[hd], 0.0) for hd in heads]
        pw = [-a for a in a_mat]
        t_mat = [eye + x for x in pw]
        yield
        for _ in range(int(math.log2(c)) - 1):
            pw16 = [x.astype(BF16) for x in pw]
            pw = [_dot(x, x) for x in pw16]
            t_mat = [tm + _dot(tm.astype(BF16), x.astype(BF16)) for tm, x in zip(t_mat, pw)]
            yield
        t16 = [tm.astype(BF16) for tm in t_mat]
        for hd in heads:
            pu_scr[slot, hd] = _dot(t16[hd], (vc[hd] * b_col[hd]).astype(BF16))
            pw_scr[slot, hd] = _dot(t16[hd], (kb[hd] * eg[hd]).astype(BF16)).astype(BF16)
            pqk_scr[slot, hd] = (_dot_nt(qc[hd].astype(BF16), kc16[hd]) * decay[hd]).astype(BF16)
            pqd_scr[slot, hd] = (qc[hd] * eg[hd]).astype(BF16)
            pkd_scr[slot, hd] = (kc[hd] * jnp.exp(jnp.where(upper_rows, g_last[hd][0], g_last[hd][1])
                                                  - g_col[hd])).astype(BF16)
            for s in range(2):
                pgl_scr[slot, hd, s] = jnp.broadcast_to(jnp.exp(g_last[hd][s]), (8, LANES))
        yield

    def advance(p, slot):
        r0 = pl.multiple_of(p * pair, pair)
        zeros_c = jnp.zeros((c, GDN_DV), BF16)
        for s in range(2):
            rs = slice(s * c, (s + 1) * c)
            s_old = [s_scr[hd] for hd in heads]
            s16 = [x.astype(BF16) for x in s_old]
            v16 = [(pu_scr[slot, hd, rs, :] - _dot(pw_scr[slot, hd, rs, :], s16[hd])).astype(BF16) for hd in heads]
            yield
            vpad = [jnp.concatenate([v16[hd], zeros_c] if s == 0 else [zeros_c, v16[hd]], axis=0) for hd in heads]
            o = [_dot(pqd_scr[slot, hd, rs, :], s16[hd]) + _dot(pqk_scr[slot, hd, rs, :], vpad[hd]) for hd in heads]
            for ch, (bb, hd) in enumerate(chains):
                s_scr[ch] = s_old[ch] * pgl_scr[slot, ch, s, 0:1, :] + _dot_tn(pkd_scr[slot, ch, rs, :], v16[ch])
                o_scr[bb, pl.ds(r0 + s * c, c), hd * GDN_DV:(hd + 1) * GDN_DV] = o[ch]
            yield

    for _ in prepare(0, 0):
        pass

    def pair_body(p, carry):
        nxt = jnp.minimum(p + 1, n_pairs - 1)
        for _ in itertools.zip_longest(prepare(nxt, (p + 1) % 2), advance(p, p % 2)):
            pass
        return carry

    lax.fori_loop(0, n_pairs, pair_body, 0)

    for bb, hd in chains:
        cols = slice(hd * GDN_DV, (hd + 1) * GDN_DV)
        o = o_scr[bb, :, cols]
        ms = jnp.mean(o * o, axis=-1, keepdims=True)
        y = o * lax.rsqrt(ms + NORM_EPS) * nw_ref[...]
        y_ref[bb, :, cols] = (y * _silu(z_ref[bb, :, cols])).astype(BF16)


def _gdn(qkv, ab, z, conv_w, alog_row, dtb_row, norm_row, pad, tg, nb):
    b, lp, _ = qkv.shape
    grid = (b // nb, lp // tg)
    tile = lambda n: pl.BlockSpec((nb, tg, n), lambda bi, i: (bi, i, 0))
    kern = functools.partial(_gdn_kernel, pad=pad, tg=tg, nb=nb)
    nch = nb * GDN_HEADS
    return pl.pallas_call(
        kern,
        grid=grid,
        in_specs=[tile(GDN_CONV_CH), tile(LANES), tile(D_GDN), _const_spec(conv_w.shape),
                  _const_spec((1, LANES)), _const_spec((1, LANES)), _const_spec((1, GDN_DV))],
        out_specs=tile(D_GDN),
        out_shape=jax.ShapeDtypeStruct((b, lp, D_GDN), BF16),
        scratch_shapes=[pltpu.VMEM((nb, GDN_CONV_CH // LANES, tg + HALO, LANES), F32),
                        pltpu.VMEM((nb, GDN_CONV_CH // LANES, tg, LANES), F32),
                        pltpu.VMEM((nb, tg, LANES), F32),
                        pltpu.VMEM((nb, tg, LANES), F32),
                        pltpu.VMEM((nb, tg, D_GDN), F32),
                        pltpu.VMEM((nch, GDN_DK, GDN_DV), F32),
                        pltpu.VMEM((2, nch, 2 * GDN_CHUNK, GDN_DV), F32),
                        pltpu.VMEM((2, nch, 2 * GDN_CHUNK, GDN_DK), BF16),
                        pltpu.VMEM((2, nch, 2 * GDN_CHUNK, 2 * GDN_CHUNK), BF16),
                        pltpu.VMEM((2, nch, 2 * GDN_CHUNK, GDN_DK), BF16),
                        pltpu.VMEM((2, nch, 2 * GDN_CHUNK, GDN_DK), BF16),
                        pltpu.VMEM((2, nch, 2, 8, LANES), F32)],
        compiler_params=pltpu.CompilerParams(dimension_semantics=("arbitrary", "arbitrary"),
                                             vmem_limit_bytes=VMEM_LIMIT),
        name="gdn",
    )(qkv, ab, z, conv_w, alog_row, dtb_row, norm_row)


def _sb_kernel(q_ref, k_ref, v_ref, nw_ref, tri_ref, y_ref, qm_scr, acc_scr, run_scr, *, pad):
    i = pl.program_id(1)
    t = SB_TILE
    nh = SB_HEADS
    lane = lax.broadcasted_iota(jnp.int32, (t, LANES), 1)
    first = lane < SB_DH
    pair_cols = lambda hd: slice((hd // 2) * LANES, (hd // 2 + 1) * LANES)
    for hd in range(nh):
        qp = q_ref[0, :, pair_cols(hd)]
        zero = jnp.zeros_like(qp)
        qm_scr[hd] = jnp.where(first, qp, zero) if hd % 2 == 0 else jnp.where(first, zero, qp)
    tri = tri_ref[...]
    acc_scr[...] = jnp.zeros_like(acc_scr)
    run_scr[...] = jnp.zeros_like(run_scr)
    qi = lax.broadcasted_iota(jnp.int32, (t, t), 0)
    ki = lax.broadcasted_iota(jnp.int32, (t, t), 1)

    def blocks(js, mask):
        items = [(pl.ds(pl.multiple_of(j * t, t), t), hd) for j in js for hd in range(nh)]
        n = len(items)
        z, sp16, cs, att = ({} for _ in range(4))
        for step in range(n + 4):
            it = step
            if it < n:
                rows, hd = items[it]
                z[it] = _dot_nt(qm_scr[hd], k_ref[0, rows, pair_cols(hd)])
            it = step - 1
            if 0 <= it < n:
                zb = z[it].astype(BF16)
                sp = jnp.maximum(zb, 0.0) + jnp.log(1.0 + jnp.exp2(jnp.abs(zb) * (-LOG2E)))
                if mask is not None:
                    sp = jnp.where(mask, sp, jnp.zeros_like(sp))
                sp16[it] = sp
            it = step - 2
            if 0 <= it < n:
                cs[it] = _dot(sp16[it], tri)
            it = step - 3
            if 0 <= it < n:
                hd = items[it][1]
                run = run_scr[hd]
                e = (z[it] - cs[it]) - jnp.concatenate([run, run], axis=1)
                if mask is not None:
                    e = jnp.where(mask, e, NEG_BIG)
                att[it] = jnp.exp(e).astype(BF16)
                run_scr[hd] = run + cs[it][:, 0:1]
            it = step - 4
            if 0 <= it < n:
                rows, hd = items[it]
                acc_scr[hd] += _dot(att[it], v_ref[0, rows, pair_cols(hd)])

    blocks([i], (ki < qi) & (ki + i * t >= pad))

    n_inner = jnp.maximum(i - 1, 0)

    def inner_oct(n, carry):
        top = i - 1 - 8 * n
        blocks([top - d for d in range(8)], None)
        return carry

    lax.fori_loop(0, n_inner // 8, inner_oct, 0)
    rem = n_inner % 8

    @pl.when(rem >= 4)
    def _():
        blocks([rem, rem - 1, rem - 2, rem - 3], None)

    rem4 = rem % 4

    @pl.when(rem4 >= 2)
    def _():
        blocks([rem4, rem4 - 1], None)

    @pl.when(rem4 % 2 == 1)
    def _():
        blocks([1], None)

    @pl.when(i >= 1)
    def _():
        blocks([0], ki >= pad)

    for pr in range(nh // 2):
        o = jnp.where(first, acc_scr[2 * pr], acc_scr[2 * pr + 1])
        sq = o * o
        ss_first = jnp.sum(jnp.where(first, sq, 0.0), axis=-1, keepdims=True)
        ss_all = jnp.sum(sq, axis=-1, keepdims=True)
        ms = jnp.where(first, ss_first, ss_all - ss_first) * (1.0 / SB_DH)
        cols = slice(pr * LANES, (pr + 1) * LANES)
        y_ref[0, :, cols] = (o * lax.rsqrt(ms + NORM_EPS) * nw_ref[:, cols]).astype(BF16)


def _sb(q, k, v, norm_row, tri, pad):
    b, lp, _ = q.shape
    t = SB_TILE
    kern = functools.partial(_sb_kernel, pad=pad)
    full = pl.BlockSpec((1, lp, D_SB), lambda bi, i: (bi, 0, 0), pipeline_mode=pl.Buffered(1))
    tile = pl.BlockSpec((1, t, D_SB), lambda bi, i: (bi, i, 0))
    return pl.pallas_call(
        kern,
        grid=(b, lp // t),
        in_specs=[tile, full, full, _const_spec((1, D_SB)), _const_spec((t, t))],
        out_specs=tile,
        out_shape=jax.ShapeDtypeStruct((b, lp, D_SB), BF16),
        scratch_shapes=[pltpu.VMEM((SB_HEADS, t, LANES), BF16),
                        pltpu.VMEM((SB_HEADS, t, LANES), F32),
                        pltpu.VMEM((SB_HEADS, t, LANES), F32)],
        compiler_params=pltpu.CompilerParams(dimension_semantics=("parallel", "arbitrary"),
                                             vmem_limit_bytes=VMEM_LIMIT),
        name="sb",
    )(q, k, v, norm_row, tri)


def _out_proj_kernel(*refs, nsub, tiles_per_batch, use_head):
    src_refs, (head_ref, yg_ref, ys_ref, wg_ref, ws_ref, post_ref, pre_ref, h1_ref, u2_ref,
               h_scr) = refs[:nsub], refs[nsub:]
    _gather_rows(src_refs, head_ref, h_scr, pl.program_id(0) % tiles_per_batch == 0, use_head)
    mix = _dot(yg_ref[...], wg_ref[...]) + _dot(ys_ref[...], ws_ref[...])
    ms = jnp.mean(mix * mix, axis=-1, keepdims=True)
    h1 = h_scr[...] + mix * lax.rsqrt(ms + NORM_EPS) * post_ref[...]
    h1_ref[...] = h1
    ms1 = jnp.mean(h1 * h1, axis=-1, keepdims=True)
    u2_ref[...] = (h1 * lax.rsqrt(ms1 + NORM_EPS) * pre_ref[...]).astype(BF16)


def _out_proj(yg, ys, source, wg, ws, post_w, pre_w, tm, tiles_per_batch):
    src, maps, head, use_head = source
    rows, d = yg.shape[0], src.shape[-1]
    row_spec = lambda n: pl.BlockSpec((tm, n), lambda i: (i, 0))
    kern = functools.partial(_out_proj_kernel, nsub=len(maps), tiles_per_batch=tiles_per_batch, use_head=use_head)
    return pl.pallas_call(
        kern,
        grid=(rows // tm,),
        in_specs=[pl.BlockSpec((ROW_ALIGN, d), m) for m in maps]
                 + [_const_spec(head.shape), row_spec(D_GDN), row_spec(D_SB), _const_spec(wg.shape),
                    _const_spec(ws.shape), _const_spec((1, d)), _const_spec((1, d))],
        out_specs=[row_spec(d), row_spec(d)],
        out_shape=[jax.ShapeDtypeStruct((rows, d), F32), jax.ShapeDtypeStruct((rows, d), BF16)],
        scratch_shapes=[pltpu.VMEM((tm, d), F32)],
        compiler_params=pltpu.CompilerParams(dimension_semantics=("parallel",), vmem_limit_bytes=VMEM_LIMIT),
        name="out_proj",
    )(*([src] * len(maps)), head, yg, ys, wg, ws, post_w, pre_w)


def _ffn_kernel(u_ref, halo_ref, h1_ref, wup_ref, cw_ref, cb_ref, wd_ref, post_ref, out_ref, uext, acc,
                *, tm, tiles_per_batch, d_ff, fc, pad, final):
    i = pl.program_id(0)
    if final:
        uext[0:HALO, :] = halo_ref[...]
    else:
        keep = jnp.where(i % tiles_per_batch == 0, 0.0, 1.0).astype(BF16)
        uext[0:HALO, :] = halo_ref[...] * keep
    uext[HALO:, :] = u_ref[...]
    ue = uext[...]

    def conv(hx, cols):
        y = hx * cw_ref[FFN_CONV - 1:FFN_CONV, cols]
        for sh in range(1, FFN_CONV):
            y = y + pltpu.roll(hx, sh, axis=0) * cw_ref[FFN_CONV - 1 - sh:FFN_CONV - sh, cols]
        return y[HALO:, :] + cb_ref[:, cols]

    n_chunks = d_ff // fc
    gcols = [slice(c * fc, (c + 1) * fc) for c in range(n_chunks)]
    ucols = [slice(d_ff + c * fc, d_ff + (c + 1) * fc) for c in range(n_chunks)]
    hg, hu, act = {}, {}, {}
    for step in range(n_chunks + 2):
        c = step
        if c < n_chunks:
            hg[c] = _dot(ue, wup_ref[:, gcols[c]])
            hu[c] = _dot(ue, wup_ref[:, ucols[c]])
        c = step - 1
        if 0 <= c < n_chunks:
            gate = conv(hg.pop(c), gcols[c])
            up = conv(hu.pop(c), ucols[c])
            act[c] = (jax.nn.gelu(gate, approximate=True) * up).astype(BF16)
        c = step - 2
        if 0 <= c < n_chunks:
            part = _dot(act.pop(c), wd_ref[gcols[c], :])
            if c == 0:
                acc[...] = part
            else:
                acc[...] += part

    f = acc[...]
    ms = jnp.mean(f * f, axis=-1, keepdims=True)
    out = h1_ref[...] + f * lax.rsqrt(ms + NORM_EPS) * post_ref[...]
    if not final:
        t = (i % tiles_per_batch) * tm + lax.broadcasted_iota(jnp.int32, out.shape, 0)
        out = jnp.where(t >= pad, out, 0.0)
    out_ref[...] = out


def _ffn(u2, h1, w_up, conv_w, conv_b, w_down, post_w, tm, lp, seq, fc, pad, final):
    rows, d = h1.shape
    b = rows // lp
    d_ff = w_down.shape[0]
    if final:
        tiles, out_rows = seq // tm, b * seq
        first = lambda g: (g // tiles) * (lp // tm) + (lp - seq) // tm + g % tiles
    else:
        tiles, out_rows = lp // tm, rows
        first = lambda g: g
    kern = functools.partial(_ffn_kernel, tm=tm, tiles_per_batch=tiles, d_ff=d_ff, fc=fc, pad=pad, final=final)
    in_spec = pl.BlockSpec((tm, d), lambda g: (first(g), 0))
    halo_spec = pl.BlockSpec((HALO, d), lambda g: (jnp.maximum(first(g) * (tm // HALO) - 1, 0), 0))
    return pl.pallas_call(
        kern,
        grid=(out_rows // tm,),
        in_specs=[in_spec, halo_spec, in_spec, _const_spec(w_up.shape), _const_spec(conv_w.shape),
                  _const_spec(conv_b.shape), _const_spec(w_down.shape), _const_spec((1, d))],
        out_specs=pl.BlockSpec((tm, d), lambda g: (g, 0)),
        out_shape=jax.ShapeDtypeStruct((out_rows, d), F32),
        scratch_shapes=[pltpu.VMEM((tm + HALO, d), BF16), pltpu.VMEM((tm, d), F32)],
        compiler_params=pltpu.CompilerParams(dimension_semantics=("parallel",), vmem_limit_bytes=VMEM_LIMIT),
        name="ffn",
    )(u2, u2, h1, w_up, conv_w, conv_b, w_down, post_w)


def _layer(hp, x, meta_tokens, pad, lp, last, attn_pre_norm, w_in, gdn_conv_w, gdn_A_log, gdn_dt_bias, gdn_norm_w,
           sb_norm_w, w_out, attn_post_norm, ffn_pre_norm, w_ffn_up, ffn_conv_w, ffn_conv_b, w_ffn_down,
           ffn_post_norm):
    b, _, d = x.shape
    rows = b * lp
    tm = _pick_tile(lp, (768, 512, 256))
    nb = 2 if b % 2 == 0 else 1
    tg = _pick_tile(lp, (768 // nb, 256))
    d_ff = w_ffn_down.shape[0]
    fc = _pick_tile(d_ff, (256, 128))

    o_a = 2 * GDN_QK + D_GDN
    o_z = o_a + 2 * GDN_HEADS
    w_main = jnp.concatenate([w_in[:, :o_a], w_in[:, o_z:]], axis=1).astype(BF16)
    w_ab = jnp.pad(w_in[:, o_a:o_z], ((0, 0), (0, LANES - 2 * GDN_HEADS))).astype(BF16)
    row = lambda v: v.reshape(1, -1).astype(F32)
    lane_row = lambda v: jnp.pad(v.astype(F32), (0, LANES - v.shape[0])).reshape(1, LANES)

    source = _row_source(None if hp is None else hp.reshape(rows, d), x, meta_tokens, pad, tm, lp)
    qkvg, z, qs, ks, vs, ab = _in_proj(source, rows, row(attn_pre_norm), w_main, w_ab, tm, lp // tm)
    r3 = lambda t: t.reshape(b, lp, t.shape[-1])
    yg = _gdn(r3(qkvg), r3(ab), r3(z), gdn_conv_w.astype(F32), lane_row(gdn_A_log), lane_row(gdn_dt_bias),
              row(gdn_norm_w), pad, tg, nb)
    ti = lax.broadcasted_iota(jnp.int32, (SB_TILE, SB_TILE), 0)
    tj = lax.broadcasted_iota(jnp.int32, (SB_TILE, SB_TILE), 1)
    tri = (ti >= tj).astype(BF16)
    ys = _sb(r3(qs), r3(ks), r3(vs), row(sb_norm_w), tri, pad)
    h1, u2 = _out_proj(yg.reshape(rows, D_GDN), ys.reshape(rows, D_SB), source, w_out[:D_GDN].astype(BF16),
                       w_out[D_GDN:].astype(BF16), row(attn_post_norm), row(ffn_pre_norm), tm, lp // tm)
    seq = x.shape[1]
    tf = ROW_ALIGN
    final = last and (pad + N_META) % tf == 0 and seq % tf == 0
    out = _ffn(u2, h1, w_ffn_up.astype(BF16), ffn_conv_w.astype(F32), row(ffn_conv_b), w_ffn_down.astype(BF16),
               row(ffn_post_norm), tf, lp, seq, fc, pad, final)
    return out.reshape(b, seq if final else lp, d)


def kernel(x, meta_tokens, attn_pre_norm, w_in, gdn_conv_w, gdn_A_log, gdn_dt_bias, gdn_norm_w, sb_norm_w, w_out,
           attn_post_norm, ffn_pre_norm, w_ffn_up, ffn_conv_w, ffn_conv_b, w_ffn_down, ffn_post_norm):
    b, seq, d = x.shape
    depth = w_in.shape[0]
    length = N_META + seq
    pad = (-length) % ROW_ALIGN
    lp = length + pad
    hp = None
    if pad + N_META != ROW_ALIGN:
        meta = jnp.broadcast_to(meta_tokens[None].astype(x.dtype), (b, N_META, d))
        hp = jnp.concatenate([jnp.zeros((b, pad, d), x.dtype), meta, x], axis=1)
    for l in range(depth):
        hp = _layer(hp, x, meta_tokens, pad, lp, l == depth - 1, attn_pre_norm[l], w_in[l], gdn_conv_w[l],
                    gdn_A_log[l], gdn_dt_bias[l], gdn_norm_w[l], sb_norm_w[l], w_out[l], attn_post_norm[l],
                    ffn_pre_norm[l], w_ffn_up[l], ffn_conv_w[l], ffn_conv_b[l], w_ffn_down[l], ffn_post_norm[l])
    return hp if hp.shape[1] == seq else hp[:, pad + N_META:]
```
